```python
import math
import jax, jax.numpy as jnp
from jax import lax
import numpy as np

D_MODEL = 1024
BATCH = 8
SEQ = 2048
DEPTH = 4
DEC_BATCH = 128
DEC_SEQ = 1
PAST_LEN = 2048
PAGE_SIZE = 128

N_HEADS = 4
HEAD_DIM = 64
ATTN_WIDTH = 2 * N_HEADS * HEAD_DIM
SSM_WIDTH = D_MODEL // 2
GROUP_SIZE = 16
N_GROUPS = SSM_WIDTH // GROUP_SIZE
STATE_DIM = 64
DT_MIN = 0.001
DT_MAX = 0.1
D_FF = 2816
ROPE_THETA = 10000.0
NORM_EPS = 1e-6
Q_BLOCK = 128
IN_WIDTH = 3 * ATTN_WIDTH + SSM_WIDTH + 2 * D_MODEL

kernel_name = "diffattn_s5_macaron_hybrid_step"

F32 = jnp.float32


def rmsnorm(x, g):
    xf = x.astype(F32)
    y = xf * lax.rsqrt(jnp.mean(xf * xf, axis=-1, keepdims=True) + NORM_EPS)
    return (y * g.astype(F32)).astype(x.dtype)


def swiglu(x, w_gate, w_up, w_down):
    return (jax.nn.silu(x @ w_gate) * (x @ w_up)) @ w_down


def rope(x, pos):
    half = HEAD_DIM // 2
    inv = ROPE_THETA ** (-jnp.arange(half, dtype=F32) / half)
    ang = pos.astype(F32)[:, None] * inv[None, :]
    cos = jnp.cos(ang)[None, :, None, :]
    sin = jnp.sin(ang)[None, :, None, :]
    xf = x.astype(F32)
    x1, x2 = xf[..., :half], xf[..., half:]
    return jnp.concatenate([x1 * cos - x2 * sin, x2 * cos + x1 * sin], axis=-1).astype(x.dtype)


def in_proj(u, w_in, b_gate, pos):
    B_, L = u.shape[0], u.shape[1]
    z = u @ w_in
    q, k, v, us, gates = jnp.split(
        z, [ATTN_WIDTH, 2 * ATTN_WIDTH, 3 * ATTN_WIDTH, 3 * ATTN_WIDTH + SSM_WIDTH], axis=-1)
    q = rope(q.reshape(B_, L, 2 * N_HEADS, HEAD_DIM), pos)
    k = rope(k.reshape(B_, L, 2 * N_HEADS, HEAD_DIM), pos)
    v = v.reshape(B_, L, N_HEADS, 2 * HEAD_DIM)
    g = jax.nn.sigmoid(gates + b_gate)
    g_attn, g_ssm = jnp.split(g, 2, axis=-1)
    return q, k, v, us, g_attn, g_ssm


def diff_attn(q, k, v, q_pos, k_pos, lam):
    B_, Lq, Lk = q.shape[0], q.shape[1], k.shape[1]
    s = jnp.einsum('bqhd,bkhd->bhqk', q, k).astype(F32) * (HEAD_DIM ** -0.5)
    mask = k_pos[None, :] <= q_pos[:, None]
    s = jnp.where(mask[None, None], s, -jnp.inf)
    p = jax.nn.softmax(s, axis=-1).reshape(B_, N_HEADS, 2, Lq, Lk)
    w = p[:, :, 0] - lam * p[:, :, 1]
    return jnp.einsum('bhqk,bkhe->bqhe', w.astype(v.dtype), v)


def ssm_discretize(a_re, a_im, log_dt, b_re, b_im):
    dt = jnp.exp(log_dt.astype(F32))[:, None]
    a_re = a_re.astype(F32)
    a_im = a_im.astype(F32)
    mag = jnp.exp(a_re * dt)
    ab_re = mag * jnp.cos(a_im * dt)
    ab_im = mag * jnp.sin(a_im * dt)
    den = a_re * a_re + a_im * a_im
    nr = ab_re - 1.0
    ni = ab_im
    coef_re = ((nr * a_re + ni * a_im) / den)[..., None]
    coef_im = ((ni * a_re - nr * a_im) / den)[..., None]
    b_re = b_re.astype(F32)
    b_im = b_im.astype(F32)
    bb_re = coef_re * b_re - coef_im * b_im
    bb_im = coef_re * b_im + coef_im * b_re
    return ab_re, ab_im, bb_re, bb_im


def _cplx_combine(e1, e2):
    a1r, a1i, b1r, b1i = e1
    a2r, a2i, b2r, b2i = e2
    return (a2r * a1r - a2i * a1i,
            a2r * a1i + a2i * a1r,
            a2r * b1r - a2i * b1i + b2r,
            a2r * b1i + a2i * b1r + b2i)


def ssm_branch(us, h0_re, h0_im, disc, c_re, c_im, d, glu_w, glu_b):
    ab_re, ab_im, bb_re, bb_im = disc
    B_, L = us.shape[0], us.shape[1]
    uf = us.astype(F32)
    ug = uf.reshape(B_, L, N_GROUPS, GROUP_SIZE)
    bu_re = jnp.einsum('gpc,blgc->blgp', bb_re, ug)
    bu_im = jnp.einsum('gpc,blgc->blgp', bb_im, ug)
    h0_re = h0_re.astype(F32)
    h0_im = h0_im.astype(F32)
    bu_re = bu_re.at[:, 0].add(ab_re * h0_re - ab_im * h0_im)
    bu_im = bu_im.at[:, 0].add(ab_re * h0_im + ab_im * h0_re)
    a_re = jnp.broadcast_to(ab_re, bu_re.shape)
    a_im = jnp.broadcast_to(ab_im, bu_im.shape)
    _, _, h_re, h_im = lax.associative_scan(_cplx_combine, (a_re, a_im, bu_re, bu_im), axis=1)
    y = (jnp.einsum('gcp,blgp->blgc', c_re.astype(F32), h_re)
         - jnp.einsum('gcp,blgp->blgc', c_im.astype(F32), h_im))
    y = y.reshape(B_, L, SSM_WIDTH) + d.astype(F32) * uf
    y = jax.nn.gelu(y)
    y = y * jax.nn.sigmoid(y @ glu_w.astype(F32) + glu_b.astype(F32))
    return y.astype(us.dtype), h_re[:, -1], h_im[:, -1]


def merge_out(o, ys, g_attn, g_ssm, subln, lam_init, w_attn_out, w_ssm_out, w_out):
    B_, L = o.shape[0], o.shape[1]
    o = rmsnorm(o, subln) * (1.0 - lam_init)
    a = o.reshape(B_, L, N_HEADS * 2 * HEAD_DIM) @ w_attn_out
    s = ys @ w_ssm_out
    return (g_attn * a + g_ssm * s) @ w_out


def setup_inputs(seed: int = 0) -> dict:
    key = jax.random.key(seed)
    ks = iter(jax.random.split(key, 48))
    n_pages = PAST_LEN // PAGE_SIZE
    n_pool = (DEC_BATCH * n_pages * 5) // 4

    def nrm(shape, scale):
        return jax.random.normal(next(ks), shape, F32) * scale

    def gain(shape):
        return 1.0 + 0.01 * jax.random.normal(next(ks), shape, F32)

    inp = {}
    inp['x_prompt'] = nrm((BATCH, SEQ, D_MODEL), 1.0)
    inp['x_sample'] = nrm((DEC_BATCH, DEC_SEQ, D_MODEL), 1.0)
    inp['cache_k'] = nrm((DEPTH, n_pool, PAGE_SIZE, 2 * N_HEADS, HEAD_DIM), 1.0)
    inp['cache_v'] = nrm((DEPTH, n_pool, PAGE_SIZE, N_HEADS, 2 * HEAD_DIM), 1.0)
    inp['state_ssm_re'] = nrm((DEPTH, DEC_BATCH, N_GROUPS, STATE_DIM), 0.1)
    inp['state_ssm_im'] = nrm((DEPTH, DEC_BATCH, N_GROUPS, STATE_DIM), 0.1)
    perm = jax.random.permutation(next(ks), n_pool)
    inp['page_table'] = perm[:DEC_BATCH * n_pages].reshape(DEC_BATCH, n_pages).astype(jnp.int32)
    inp['norm_ffn1'] = gain((DEPTH, D_MODEL))
    inp['ffn1_w_gate'] = nrm((DEPTH, D_MODEL, D_FF), D_MODEL ** -0.5)
    inp['ffn1_w_up'] = nrm((DEPTH, D_MODEL, D_FF), D_MODEL ** -0.5)
    inp['ffn1_w_down'] = nrm((DEPTH, D_FF, D_MODEL), D_FF ** -0.5)
    inp['norm_mix'] = gain((DEPTH, D_MODEL))
    inp['w_in'] = nrm((DEPTH, D_MODEL, IN_WIDTH), D_MODEL ** -0.5)
    inp['b_gate'] = nrm((DEPTH, 2 * D_MODEL), 0.02)
    inp['lambda_q1'] = nrm((DEPTH, HEAD_DIM), 0.1)
    inp['lambda_k1'] = nrm((DEPTH, HEAD_DIM), 0.1)
    inp['lambda_q2'] = nrm((DEPTH, HEAD_DIM), 0.1)
    inp['lambda_k2'] = nrm((DEPTH, HEAD_DIM), 0.1)
    inp['attn_subln'] = gain((DEPTH, 2 * HEAD_DIM))
    inp['w_attn_out'] = nrm((DEPTH, N_HEADS * 2 * HEAD_DIM, D_MODEL), (N_HEADS * 2 * HEAD_DIM) ** -0.5)
    n_idx = jnp.arange(STATE_DIM, dtype=F32)
    inp['ssm_a_re'] = -0.5 + nrm((DEPTH, N_GROUPS, STATE_DIM), 0.01)
    inp['ssm_a_im'] = jnp.pi * n_idx[None, None, :] + nrm((DEPTH, N_GROUPS, STATE_DIM), 0.01)
    inp['ssm_log_dt'] = jax.random.uniform(next(ks), (DEPTH, N_GROUPS), F32,
                                           math.log(DT_MIN), math.log(DT_MAX))
    inp['ssm_b_re'] = nrm((DEPTH, N_GROUPS, STATE_DIM, GROUP_SIZE), (2 * GROUP_SIZE) ** -0.5)
    inp['ssm_b_im'] = nrm((DEPTH, N_GROUPS, STATE_DIM, GROUP_SIZE), (2 * GROUP_SIZE) ** -0.5)
    inp['ssm_c_re'] = nrm((DEPTH, N_GROUPS, GROUP_SIZE, STATE_DIM), (2 * STATE_DIM) ** -0.5)
    inp['ssm_c_im'] = nrm((DEPTH, N_GROUPS, GROUP_SIZE, STATE_DIM), (2 * STATE_DIM) ** -0.5)
    inp['ssm_d'] = nrm((DEPTH, SSM_WIDTH), 1.0)
    inp['glu_w'] = nrm((DEPTH, SSM_WIDTH, SSM_WIDTH), SSM_WIDTH ** -0.5)
    inp['glu_b'] = nrm((DEPTH, SSM_WIDTH), 0.02)
    inp['w_ssm_out'] = nrm((DEPTH, SSM_WIDTH, D_MODEL), SSM_WIDTH ** -0.5)
    inp['w_out'] = nrm((DEPTH, D_MODEL, D_MODEL), D_MODEL ** -0.5)
    inp['norm_ffn2'] = gain((DEPTH, D_MODEL))
    inp['ffn2_w_gate'] = nrm((DEPTH, D_MODEL, D_FF), D_MODEL ** -0.5)
    inp['ffn2_w_up'] = nrm((DEPTH, D_MODEL, D_FF), D_MODEL ** -0.5)
    inp['ffn2_w_down'] = nrm((DEPTH, D_FF, D_MODEL), D_FF ** -0.5)
    inp['final_norm'] = gain((D_MODEL,))
    return inp


def reference(x_prompt, x_sample, cache_k, cache_v, state_ssm_re, state_ssm_im, page_table,
              norm_ffn1, ffn1_w_gate, ffn1_w_up, ffn1_w_down, norm_mix, w_in, b_gate,
              lambda_q1, lambda_k1, lambda_q2, lambda_k2, attn_subln, w_attn_out,
              ssm_a_re, ssm_a_im, ssm_log_dt, ssm_b_re, ssm_b_im, ssm_c_re, ssm_c_im, ssm_d,
              glu_w, glu_b, w_ssm_out, w_out, norm_ffn2, ffn2_w_gate, ffn2_w_up, ffn2_w_down,
              final_norm):
    n_blocks = SEQ // Q_BLOCK
    pos_p = jnp.arange(SEQ, dtype=jnp.int32)
    pos_p_blocks = pos_p.reshape(n_blocks, Q_BLOCK)
    pos_s = PAST_LEN + jnp.arange(DEC_SEQ, dtype=jnp.int32)
    k_pos_s = jnp.arange(PAST_LEN + DEC_SEQ, dtype=jnp.int32)
    h0_p = jnp.zeros((BATCH, N_GROUPS, STATE_DIM), F32)

    xp, xs = x_prompt, x_sample
    kp_l, vp_l, hpr_l, hpi_l = [], [], [], []
    ks_l, vs_l, hsr_l, hsi_l = [], [], [], []
    for l in range(DEPTH):
        lam_init = 0.8 - 0.6 * math.exp(-0.3 * l)
        lam = (jnp.exp(jnp.sum(lambda_q1[l].astype(F32) * lambda_k1[l].astype(F32)))
               - jnp.exp(jnp.sum(lambda_q2[l].astype(F32) * lambda_k2[l].astype(F32)))
               + lam_init)
        disc = ssm_discretize(ssm_a_re[l], ssm_a_im[l], ssm_log_dt[l], ssm_b_re[l], ssm_b_im[l])

        h = xp + 0.5 * swiglu(rmsnorm(xp, norm_ffn1[l]), ffn1_w_gate[l], ffn1_w_up[l], ffn1_w_down[l])
        q, k, v, us, ga, gs = in_proj(rmsnorm(h, norm_mix[l]), w_in[l], b_gate[l], pos_p)
        qb = q.reshape(BATCH, n_blocks, Q_BLOCK, 2 * N_HEADS, HEAD_DIM).swapaxes(0, 1)
        ob = lax.map(lambda a: diff_attn(a[0], k, v, a[1], pos_p, lam), (qb, pos_p_blocks))
        o = ob.swapaxes(0, 1).reshape(BATCH, SEQ, N_HEADS, 2 * HEAD_DIM)
        ys, hr, hi = ssm_branch(us, h0_p, h0_p, disc, ssm_c_re[l], ssm_c_im[l], ssm_d[l], glu_w[l], glu_b[l])
        h = h + merge_out(o, ys, ga, gs, attn_subln[l], lam_init, w_attn_out[l], w_ssm_out[l], w_out[l])
        xp = h + 0.5 * swiglu(rmsnorm(h, norm_ffn2[l]), ffn2_w_gate[l], ffn2_w_up[l], ffn2_w_down[l])
        kp_l.append(k)
        vp_l.append(v)
        hpr_l.append(hr)
        hpi_l.append(hi)

        h = xs + 0.5 * swiglu(rmsnorm(xs, norm_ffn1[l]), ffn1_w_gate[l], ffn1_w_up[l], ffn1_w_down[l])
        q, k, v, us, ga, gs = in_proj(rmsnorm(h, norm_mix[l]), w_in[l], b_gate[l], pos_s)
        k_past = cache_k[l][page_table].reshape(DEC_BATCH, -1, 2 * N_HEADS, HEAD_DIM)
        v_past = cache_v[l][page_table].reshape(DEC_BATCH, -1, N_HEADS, 2 * HEAD_DIM)
        k_all = jnp.concatenate([k_past.astype(k.dtype), k], axis=1)
        v_all = jnp.concatenate([v_past.astype(v.dtype), v], axis=1)
        o = diff_attn(q, k_all, v_all, pos_s, k_pos_s, lam)
        ys, hr, hi = ssm_branch(us, state_ssm_re[l], state_ssm_im[l], disc,
                                ssm_c_re[l], ssm_c_im[l], ssm_d[l], glu_w[l], glu_b[l])
        h = h + merge_out(o, ys, ga, gs, attn_subln[l], lam_init, w_attn_out[l], w_ssm_out[l], w_out[l])
        xs = h + 0.5 * swiglu(rmsnorm(h, norm_ffn2[l]), ffn2_w_gate[l], ffn2_w_up[l], ffn2_w_down[l])
        ks_l.append(k)
        vs_l.append(v)
        hsr_l.append(hr)
        hsi_l.append(hi)

    y_prompt = rmsnorm(xp, final_norm)
    y_sample = rmsnorm(xs, final_norm)
    return (y_prompt, y_sample,
            jnp.stack(kp_l), jnp.stack(vp_l), jnp.stack(hpr_l), jnp.stack(hpi_l),
            jnp.stack(ks_l), jnp.stack(vs_l), jnp.stack(hsr_l), jnp.stack(hsi_l))
```

```python
import functools
import math

import jax
import jax.numpy as jnp
from jax import lax
from jax.experimental import pallas as pl
from jax.experimental.pallas import tpu as pltpu

F32 = jnp.float32
BF16 = jnp.bfloat16

D_MODEL = 1024
N_HEADS = 4
HEAD_DIM = 64
ATTN_WIDTH = 2 * N_HEADS * HEAD_DIM
SSM_WIDTH = 512
GROUP_SIZE = 16
N_GROUPS = SSM_WIDTH // GROUP_SIZE
STATE_DIM = 64
D_FF = 2816
ROPE_THETA = 10000.0
NORM_EPS = 1e-6
PAGE_SIZE = 128

LANES = 128
SG_GROUPS = LANES // GROUP_SIZE
N_SG = N_GROUPS // SG_GROUPS
SG_STATE = SG_GROUPS * STATE_DIM
T_CHUNK = 8
N_DEC_ROWS = 16
FF_CHUNK = 256
VMEM_LIMIT = 56 * 2**20

_HI = lax.Precision.HIGHEST


def _dot(a, b):
    return jnp.dot(a, b, preferred_element_type=F32)


def _rms(x, g):
    return x * lax.rsqrt(jnp.mean(x * x, axis=-1, keepdims=True) + NORM_EPS) * g


def _params(*sem):
    return pltpu.CompilerParams(dimension_semantics=sem, vmem_limit_bytes=VMEM_LIMIT)


def _const_spec(shape, index):
    return pl.BlockSpec(shape, lambda *_: index, pipeline_mode=pl.Buffered(1))


def _ffn_body(*refs, final):
    if final:
        x_ref, g_ref, wg_ref, wu_ref, wd_ref, fg_ref, o_ref = refs
    else:
        x_ref, g_ref, wg_ref, wu_ref, wd_ref, o_ref = refs
    x = x_ref[...]
    u = _rms(x, g_ref[...]).astype(BF16)
    acc = jnp.zeros(x.shape, F32)
    for c in range(D_FF // FF_CHUNK):
        cols = slice(c * FF_CHUNK, (c + 1) * FF_CHUNK)
        gate = _dot(u, wg_ref[:, cols])
        up = _dot(u, wu_ref[:, cols])
        act = (gate * jax.nn.sigmoid(gate) * up).astype(BF16)
        acc = acc + _dot(act, wd_ref[cols, :])
    y = x + 0.5 * acc
    if final:
        y = _rms(y, fg_ref[...])
    o_ref[...] = y


def _ffn(x, layer, g, wg, wu, wd, final_g=None, tm=512):
    m = x.shape[0]
    tm = min(tm, m)
    in_specs = [
        pl.BlockSpec((tm, D_MODEL), lambda i: (i, 0)),
        _const_spec((None, 1, D_MODEL), (layer, 0, 0)),
        _const_spec((None, D_MODEL, D_FF), (layer, 0, 0)),
        _const_spec((None, D_MODEL, D_FF), (layer, 0, 0)),
        _const_spec((None, D_FF, D_MODEL), (layer, 0, 0)),
    ]
    args = [x, g, wg, wu, wd]
    if final_g is not None:
        in_specs.append(_const_spec((1, D_MODEL), (0, 0)))
        args.append(final_g)
    return pl.pallas_call(
        functools.partial(_ffn_body, final=final_g is not None),
        grid=(m // tm,),
        in_specs=in_specs,
        out_specs=pl.BlockSpec((tm, D_MODEL), lambda i: (i, 0)),
        out_shape=jax.ShapeDtypeStruct((m, D_MODEL), F32),
        compiler_params=_params("parallel"),
        name="ffn",
    )(*args)


def _swap_halves_lanes(x):
    n = x.shape[-1]
    half = HEAD_DIM // 2
    lane = lax.broadcasted_iota(jnp.int32, x.shape, x.ndim - 1)
    return jnp.where(lane % HEAD_DIM < half, pltpu.roll(x, n - half, x.ndim - 1),
                     pltpu.roll(x, half, x.ndim - 1))


def _swap_halves_rows(x):
    half = HEAD_DIM // 2
    parts = []
    for r in range(0, x.shape[0], HEAD_DIM):
        parts += [x[r + half:r + HEAD_DIM], x[r:r + half]]
    return jnp.concatenate(parts, axis=0)


def _inproj_body(*refs, k_rows):
    n_in = 11 if k_rows else 10
    x_ref, g_ref, wq_ref, wkt_ref, wv_ref, wus_ref, cos_ref, sin_ref, cost_ref, sint_ref = refs[:10]
    q_ref, kt_ref, ktb_ref, v4_ref, vb_ref, us_ref = refs[n_in:n_in + 6]
    tm = x_ref.shape[0]
    u = _rms(x_ref[...], g_ref[...]).astype(BF16)
    cos = cos_ref[...]
    sin = sin_ref[...]
    q = _dot(u, wq_ref[...])
    q = q * cos + _swap_halves_lanes(q) * sin
    q_ref[...] = (q * (HEAD_DIM ** -0.5)).astype(BF16)
    kt = lax.dot_general(wkt_ref[...], u, (((1,), (1,)), ((), ())), preferred_element_type=F32)
    kt = kt * cost_ref[...] + _swap_halves_rows(kt) * sint_ref[...]
    kt_ref[...] = kt
    ktb_ref[...] = kt.astype(BF16)
    v = _dot(u, wv_ref[...])
    vb_ref[...] = v.astype(BF16)
    for h in range(N_HEADS):
        v4_ref[pl.ds(h, tm, stride=N_HEADS), :] = v[:, h * LANES:(h + 1) * LANES]
    us_ref[...] = _dot(u, wus_ref[...])
    if k_rows:
        wk_ref, k_ref = refs[10], refs[n_in + 6]
        k = _dot(u, wk_ref[...])
        k_ref[...] = k * cos + _swap_halves_lanes(k) * sin


def _inproj(x, layer, g, w_in, w_kt, tabs, k_rows=False, tm=512):
    b, l, _ = x.shape
    tm = min(tm, l)
    cos, sin, cost, sint = tabs
    aw = ATTN_WIDTH
    wcol = lambda c: _const_spec((None, D_MODEL, aw), (layer, 0, c))
    in_specs = [
        pl.BlockSpec((None, tm, D_MODEL), lambda i, j: (i, j, 0)),
        _const_spec((None, 1, D_MODEL), (layer, 0, 0)),
        wcol(0),
        _const_spec((None, aw, D_MODEL), (layer, 0, 0)),
        wcol(2),
        wcol(3),
        pl.BlockSpec((tm, aw), lambda i, j: (j, 0)),
        pl.BlockSpec((tm, aw), lambda i, j: (j, 0)),
        pl.BlockSpec((aw, tm), lambda i, j: (0, j)),
        pl.BlockSpec((aw, tm), lambda i, j: (0, j)),
    ]
    args = [x, g, w_in, w_kt, w_in, w_in, cos, sin, cost, sint]
    row_spec = pl.BlockSpec((None, tm, aw), lambda i, j: (i, j, 0))
    col_spec = pl.BlockSpec((None, aw, tm), lambda i, j: (i, 0, j))
    out_specs = [row_spec, col_spec, col_spec,
                 pl.BlockSpec((None, tm * N_HEADS, LANES), lambda i, j: (i, j, 0)),
                 row_spec, row_spec]
    out_shape = [jax.ShapeDtypeStruct((b, l, aw), BF16),
                 jax.ShapeDtypeStruct((b, aw, l), F32),
                 jax.ShapeDtypeStruct((b, aw, l), BF16),
                 jax.ShapeDtypeStruct((b, l * N_HEADS, LANES), F32),
                 jax.ShapeDtypeStruct((b, l, aw), BF16),
                 jax.ShapeDtypeStruct((b, l, SSM_WIDTH), F32)]
    if k_rows:
        in_specs.append(wcol(1))
        args.append(w_in)
        out_specs.append(row_spec)
        out_shape.append(jax.ShapeDtypeStruct((b, l, aw), F32))
    return pl.pallas_call(
        functools.partial(_inproj_body, k_rows=k_rows),
        grid=(b, l // tm),
        in_specs=in_specs,
        out_specs=out_specs,
        out_shape=out_shape,
        compiler_params=_params("parallel", "parallel"),
        name="in_proj",
    )(*args)


def _lambda(lam_ref, lam_init):
    a = jnp.sum(lam_ref[0:1, :] * lam_ref[1:2, :], axis=-1, keepdims=True)
    b = jnp.sum(lam_ref[2:3, :] * lam_ref[3:4, :], axis=-1, keepdims=True)
    return jnp.exp(a) - jnp.exp(b) + lam_init


def _head_norm(o, subln, lam_init):
    return _rms(o, subln) * (1.0 - lam_init)


def _attn_body(lam_ref, subln_ref, q_ref, kt_ref, v_ref, o_ref, *, lam_init, tq, tk):
    i = pl.program_id(2)
    q = q_ref[...]
    lane = lax.broadcasted_iota(jnp.int32, q.shape, 1)
    zero = jnp.zeros_like(q)
    qq = jnp.concatenate([jnp.where(lane < HEAD_DIM, q, zero), jnp.where(lane >= HEAD_DIM, q, zero)], axis=0)

    def step(j, carry, diagonal):
        m, l, acc = carry
        start = pl.multiple_of(j * tk, tk)
        s = _dot(qq, kt_ref[:, pl.ds(start, tk)])
        if diagonal:
            row = lax.broadcasted_iota(jnp.int32, s.shape, 0) % tq
            col = lax.broadcasted_iota(jnp.int32, s.shape, 1)
            s = jnp.where(col <= row, s, -jnp.inf)
        m_new = jnp.maximum(m, jnp.max(s, axis=-1, keepdims=True))
        alpha = jnp.exp(m - m_new)
        p = jnp.exp(s - m_new)
        l = alpha * l + jnp.sum(p, axis=-1, keepdims=True)
        acc = alpha * acc + _dot(p.astype(BF16), v_ref[pl.ds(start, tk), :])
        return m_new, l, acc

    init = (jnp.full((2 * tq, 1), -jnp.inf, F32), jnp.zeros((2 * tq, 1), F32),
            jnp.zeros((2 * tq, LANES), F32))
    carry = lax.fori_loop(0, i, functools.partial(step, diagonal=False), init)
    _, l, acc = step(i, carry, True)
    o = acc / l
    lam = _lambda(lam_ref, lam_init)
    o = o[:tq] - lam * o[tq:]
    o_ref[...] = _head_norm(o, subln_ref[...], lam_init).astype(o_ref.dtype)


def _attn_prompt(q, ktb, vb, layer, lam_rows, subln, lam_init, tq=256):
    b, l, _ = q.shape
    return pl.pallas_call(
        functools.partial(_attn_body, lam_init=lam_init, tq=tq, tk=tq),
        grid=(b, N_HEADS, l // tq),
        in_specs=[
            _const_spec((None, 4, HEAD_DIM), (layer, 0, 0)),
            _const_spec((None, 1, LANES), (layer, 0, 0)),
            pl.BlockSpec((None, tq, LANES), lambda i, h, j: (i, j, h)),
            pl.BlockSpec((None, LANES, l), lambda i, h, j: (i, h, 0)),
            pl.BlockSpec((None, l, LANES), lambda i, h, j: (i, 0, h)),
        ],
        out_specs=pl.BlockSpec((None, tq, LANES), lambda i, h, j: (i, j, h)),
        out_shape=jax.ShapeDtypeStruct((b, l, ATTN_WIDTH), BF16),
        compiler_params=_params("parallel", "parallel", "arbitrary"),
        name="attn_prompt",
    )(lam_rows, subln, q, ktb, vb)


def _decode_body(pt_ref, lam_ref, subln_ref, q_ref, knew_ref, vnew_ref, *refs, lam_init, n_pages):
    del pt_ref
    k_refs = refs[:n_pages]
    v_refs = refs[n_pages:2 * n_pages]
    o_ref = refs[2 * n_pages]
    n_sub = 2 * N_HEADS
    row = lax.broadcasted_iota(jnp.int32, (n_sub, ATTN_WIDTH), 0)
    lane = lax.broadcasted_iota(jnp.int32, (n_sub, ATTN_WIDTH), 1)
    sub = jnp.where(row < N_HEADS, 2 * row, 2 * (row - N_HEADS) + 1)
    q_blk = jnp.where(lane // HEAD_DIM == sub, q_ref[...].astype(F32), 0.0)
    q_bf = q_blk.astype(BF16)
    s_new = jnp.sum(q_blk * knew_ref[...], axis=-1, keepdims=True)
    scores = [_dot(q_bf, k_refs[p][...].astype(BF16)) for p in range(n_pages)]
    m = s_new
    for s in scores:
        m = jnp.maximum(m, jnp.max(s, axis=-1, keepdims=True))
    p_new = jnp.exp(s_new - m)
    probs = [jnp.exp(s - m) for s in scores]
    l = p_new
    for p in probs:
        l = l + jnp.sum(p, axis=-1, keepdims=True)
    lam = _lambda(lam_ref, lam_init)
    inv = 1.0 / l
    r8 = lax.broadcasted_iota(jnp.int32, (n_sub, 1), 0)
    coef = jnp.where(r8 < N_HEADS, inv, -lam * inv)

    def fold(x):
        y = x * coef
        y = y + pltpu.roll(y, N_HEADS, 0)
        return jnp.where(lax.broadcasted_iota(jnp.int32, y.shape, 0) < N_HEADS, y, 0.0)

    w_new = fold(jnp.broadcast_to(p_new, (n_sub, LANES)))
    vnew = vnew_ref[...]
    vnew8 = jnp.concatenate([vnew, jnp.zeros_like(vnew)], axis=0)
    acc = w_new * vnew8
    rsel = lax.broadcasted_iota(jnp.int32, (n_sub, PAGE_SIZE), 0)
    for p in range(n_pages):
        w = fold(probs[p])
        for h in range(N_HEADS):
            wh = jnp.where(rsel == h, w, 0.0).astype(BF16)
            vh = v_refs[p][pl.ds(h, PAGE_SIZE, stride=N_HEADS), :].astype(BF16)
            acc = acc + _dot(wh, vh)
    o = _head_norm(acc[:N_HEADS], subln_ref[...], lam_init)
    o_ref[...] = o.astype(o_ref.dtype)


def _attn_decode(q, k_new, v_new4, cache_kt, cache_v, page_table, layer, lam_rows, subln, lam_init):
    b = q.shape[0]
    n_pages = page_table.shape[1]
    page_spec = lambda p: pl.BlockSpec((None, None, ATTN_WIDTH, LANES),
                                       lambda i, pt: (layer, pt[i, p], 0, 0))
    in_specs = [
        pl.BlockSpec((None, 4, HEAD_DIM), lambda i, pt: (layer, 0, 0)),
        pl.BlockSpec((None, 1, LANES), lambda i, pt: (layer, 0, 0)),
        pl.BlockSpec((None, 1, ATTN_WIDTH), lambda i, pt: (i, 0, 0)),
        pl.BlockSpec((None, 1, ATTN_WIDTH), lambda i, pt: (i, 0, 0)),
        pl.BlockSpec((None, N_HEADS, LANES), lambda i, pt: (i, 0, 0)),
    ] + [page_spec(p) for p in range(n_pages)] * 2
    grid_spec = pltpu.PrefetchScalarGridSpec(
        num_scalar_prefetch=1,
        grid=(b,),
        in_specs=in_specs,
        out_specs=pl.BlockSpec((None, N_HEADS, LANES), lambda i, pt: (i, 0, 0)),
    )
    return pl.pallas_call(
        functools.partial(_decode_body, lam_init=lam_init, n_pages=n_pages),
        grid_spec=grid_spec,
        out_shape=jax.ShapeDtypeStruct((b, N_HEADS, LANES), BF16),
        compiler_params=_params("arbitrary"),
        name="attn_decode",
    )(page_table, lam_rows, subln, q, k_new, v_new4,
      *([cache_kt] * n_pages), *([cache_v] * n_pages))


def _cmul(ar, ai, br, bi):
    return ar * br - ai * bi, ar * bi + ai * br


def _ssm_prep_body(are_ref, aim_ref, ldt_ref, btr_ref, bti_ref, ctr_ref, cti_ref,
                   wk_ref, ws_ref, wc_ref, wc0_ref, dec_ref):
    t = T_CHUNK
    a_re = are_ref[...]
    a_im = aim_ref[...]
    dt = jnp.exp(ldt_ref[...])
    mag = jnp.exp(a_re * dt)
    p_re = mag * jnp.cos(a_im * dt)
    p_im = mag * jnp.sin(a_im * dt)
    den = a_re * a_re + a_im * a_im
    nr = p_re - 1.0
    coef_re = (nr * a_re + p_im * a_im) / den
    coef_im = (p_im * a_re - nr * a_im) / den

    r = lax.broadcasted_iota(jnp.int32, (LANES, SG_STATE), 0)
    c = lax.broadcasted_iota(jnp.int32, (LANES, SG_STATE), 1)
    bt_mask = r // GROUP_SIZE == c // STATE_DIM
    r = lax.broadcasted_iota(jnp.int32, (SG_STATE, LANES), 0)
    c = lax.broadcasted_iota(jnp.int32, (SG_STATE, LANES), 1)
    ct_mask = r // STATE_DIM == c // GROUP_SIZE
    btr = jnp.where(bt_mask, btr_ref[...], 0.0)
    bti = jnp.where(bt_mask, bti_ref[...], 0.0)
    ctr = jnp.where(ct_mask, ctr_ref[...], 0.0)
    cti = jnp.where(ct_mask, cti_ref[...], 0.0)
    bbr, bbi = _cmul(coef_re, coef_im, btr, bti)

    pow_re = [jnp.ones_like(p_re)]
    pow_im = [jnp.zeros_like(p_im)]
    col = lambda x: jnp.transpose(jnp.broadcast_to(x, (LANES, SG_STATE)))
    p_re_c, p_im_c = col(p_re), col(p_im)
    cpow_re = [jnp.ones_like(p_re_c)]
    cpow_im = [jnp.zeros_like(p_im_c)]
    for _ in range(t):
        nr_, ni_ = _cmul(pow_re[-1], pow_im[-1], p_re, p_im)
        pow_re.append(nr_)
        pow_im.append(ni_)
        nr_, ni_ = _cmul(cpow_re[-1], cpow_im[-1], p_re_c, p_im_c)
        cpow_re.append(nr_)
        cpow_im.append(ni_)

    ct_cat = jnp.concatenate([ctr, cti], axis=0)
    wk_ref[...] = jnp.zeros(wk_ref.shape, wk_ref.dtype)
    for j in range(t):
        er, ei = _cmul(bbr, bbi, pow_re[j], pow_im[j])
        kj = jnp.dot(jnp.concatenate([er, -ei], axis=1), ct_cat, precision=_HI,
                     preferred_element_type=F32).astype(BF16)
        for t_in in range(t - j):
            wk_ref[t_in * LANES:(t_in + 1) * LANES, (t_in + j) * LANES:(t_in + j + 1) * LANES] = kj
        rows = slice((t - 1 - j) * LANES, (t - j) * LANES)
        ws_ref[rows, :SG_STATE] = er.astype(BF16)
        ws_ref[rows, SG_STATE:] = ei.astype(BF16)
    for j in range(1, t + 1):
        cols = slice((j - 1) * LANES, j * LANES)
        wc_ref[:SG_STATE, cols] = (ctr * cpow_re[j] - cti * cpow_im[j]).astype(BF16)
        wc_ref[SG_STATE:, cols] = (-(ctr * cpow_im[j] + cti * cpow_re[j])).astype(BF16)
    wc0_ref[:SG_STATE, :] = ctr.astype(BF16)
    wc0_ref[SG_STATE:, :] = (-cti).astype(BF16)

    dec_rows = [jnp.concatenate([p_re, p_im], axis=1)]
    dr, di = pow_re[t], pow_im[t]
    for _ in range(N_DEC_ROWS - 1):
        dec_rows.append(jnp.concatenate([dr, di], axis=1))
        dr, di = _cmul(dr, di, dr, di)
    dec_ref[...] = jnp.concatenate(dec_rows, axis=0)


def _ssm_prep(a_re, a_im, log_dt, b_re, b_im, c_re, c_im):
    depth = a_re.shape[0]
    rowed = lambda x: x.reshape(depth, N_SG, 1, SG_STATE)
    ldt = jnp.broadcast_to(log_dt[:, :, None], (depth, N_GROUPS, STATE_DIM))

    def bt(x):
        y = x.reshape(depth, N_SG, SG_GROUPS, STATE_DIM, GROUP_SIZE).transpose(0, 1, 4, 2, 3)
        y = y.reshape(depth, N_SG, 1, GROUP_SIZE, SG_STATE)
        return jnp.broadcast_to(y, (depth, N_SG, SG_GROUPS, GROUP_SIZE, SG_STATE)).reshape(
            depth, N_SG, LANES, SG_STATE)

    def ct(x):
        y = x.reshape(depth, N_SG, SG_GROUPS, GROUP_SIZE, STATE_DIM).transpose(0, 1, 4, 2, 3)
        y = y.reshape(depth, N_SG, 1, STATE_DIM, LANES)
        return jnp.broadcast_to(y, (depth, N_SG, SG_GROUPS, STATE_DIM, LANES)).reshape(
            depth, N_SG, SG_STATE, LANES)

    tk = T_CHUNK * LANES
    blk = lambda *shape: pl.BlockSpec((None, None) + shape, lambda l, g: (l, g, 0, 0))
    return pl.pallas_call(
        _ssm_prep_body,
        grid=(depth, N_SG),
        in_specs=[blk(1, SG_STATE)] * 3 + [blk(LANES, SG_STATE)] * 2 + [blk(SG_STATE, LANES)] * 2,
        out_specs=[blk(tk, tk), blk(tk, 2 * SG_STATE), blk(2 * SG_STATE, tk),
                   blk(2 * SG_STATE, LANES), blk(N_DEC_ROWS, 2 * SG_STATE)],
        out_shape=[jax.ShapeDtypeStruct((depth, N_SG, tk, tk), BF16),
                   jax.ShapeDtypeStruct((depth, N_SG, tk, 2 * SG_STATE), BF16),
                   jax.ShapeDtypeStruct((depth, N_SG, 2 * SG_STATE, tk), BF16),
                   jax.ShapeDtypeStruct((depth, N_SG, 2 * SG_STATE, LANES), BF16),
                   jax.ShapeDtypeStruct((depth, N_SG, N_DEC_ROWS, 2 * SG_STATE), F32)],
        compiler_params=_params("parallel", "parallel"),
        name="ssm_prep",
    )(rowed(a_re), rowed(a_im), rowed(ldt), bt(b_re), bt(b_im), ct(c_re), ct(c_im))


def _ssm_prompt_body(us_ref, wk_ref, ws_ref, wc_ref, dec_ref, y_ref, st_ref):
    t = T_CHUNK
    n_chunks = us_ref.shape[0] // t
    x = jnp.concatenate([us_ref[pl.ds(i, n_chunks, stride=t), :] for i in range(t)], axis=1).astype(BF16)
    h_loc = _dot(x, ws_ref[...])
    s_re = h_loc[:, :SG_STATE]
    s_im = h_loc[:, SG_STATE:]
    row = lax.broadcasted_iota(jnp.int32, s_re.shape, 0)
    shifted = lambda v, d: jnp.where(row >= d, pltpu.roll(v, d, 0), 0.0)
    d = 1
    i = 1
    while d < n_chunks:
        a_r = dec_ref[i:i + 1, :SG_STATE]
        a_i = dec_ref[i:i + 1, SG_STATE:]
        add_re, add_im = _cmul(a_r, a_i, shifted(s_re, d), shifted(s_im, d))
        s_re = s_re + add_re
        s_im = s_im + add_im
        d *= 2
        i += 1
    st_ref[0:1, :] = s_re[n_chunks - 1:n_chunks, :]
    st_ref[1:2, :] = s_im[n_chunks - 1:n_chunks, :]
    h_in = jnp.concatenate([shifted(s_re, 1), shifted(s_im, 1)], axis=1).astype(BF16)
    y = _dot(x, wk_ref[...]) + _dot(h_in, wc_ref[...])
    for i in range(t):
        y_ref[pl.ds(i, n_chunks, stride=t), :] = y[:, i * LANES:(i + 1) * LANES]


def _ssm_prompt(us, layer, mats):
    wk, ws, wc, _, dec = mats
    b, l, _ = us.shape
    assert l // T_CHUNK <= 2 ** (N_DEC_ROWS - 1)
    tk = T_CHUNK * LANES
    mat = lambda r, c: pl.BlockSpec((None, None, r, c), lambda g, i: (layer, g, 0, 0))
    return pl.pallas_call(
        _ssm_prompt_body,
        grid=(N_SG, b),
        in_specs=[pl.BlockSpec((None, l, LANES), lambda g, i: (i, 0, g)),
                  mat(tk, tk), mat(tk, 2 * SG_STATE), mat(2 * SG_STATE, tk),
                  mat(N_DEC_ROWS, 2 * SG_STATE)],
        out_specs=[pl.BlockSpec((None, l, LANES), lambda g, i: (i, 0, g)),
                   pl.BlockSpec((None, None, 2, SG_STATE), lambda g, i: (i, g, 0, 0))],
        out_shape=[jax.ShapeDtypeStruct((b, l, SSM_WIDTH), F32),
                   jax.ShapeDtypeStruct((b, N_SG, 2, SG_STATE), F32)],
        compiler_params=_params("parallel", "parallel"),
        name="ssm_prompt",
    )(us, wk, ws, wc, dec)


def _ssm_sample_body(us_ref, h0r_ref, h0i_ref, ws_ref, wc0_ref, dec_ref, y_ref, hr_ref, hi_ref):
    bu = _dot(us_ref[...].astype(BF16), ws_ref[...])
    a_r = dec_ref[0:1, :SG_STATE]
    a_i = dec_ref[0:1, SG_STATE:]
    dr, di = _cmul(a_r, a_i, h0r_ref[...], h0i_ref[...])
    h_re = bu[:, :SG_STATE] + dr
    h_im = bu[:, SG_STATE:] + di
    hr_ref[...] = h_re
    hi_ref[...] = h_im
    y_ref[...] = _dot(jnp.concatenate([h_re, h_im], axis=1).astype(BF16), wc0_ref[...])


def _ssm_sample(us, h0_re, h0_im, layer, mats):
    _, ws, _, wc0, dec = mats
    b = us.shape[0]
    st_spec = pl.BlockSpec((None, b, SG_STATE), lambda g: (g, 0, 0))
    return pl.pallas_call(
        _ssm_sample_body,
        grid=(N_SG,),
        in_specs=[pl.BlockSpec((b, LANES), lambda g: (0, g)), st_spec, st_spec,
                  pl.BlockSpec((None, None, LANES, 2 * SG_STATE), lambda g: (layer, g, T_CHUNK - 1, 0)),
                  pl.BlockSpec((None, None, 2 * SG_STATE, LANES), lambda g: (layer, g, 0, 0)),
                  pl.BlockSpec((None, None, N_DEC_ROWS, 2 * SG_STATE), lambda g: (layer, g, 0, 0))],
        out_specs=[pl.BlockSpec((b, LANES), lambda g: (0, g)), st_spec, st_spec],
        out_shape=[jax.ShapeDtypeStruct((b, SSM_WIDTH), F32),
                   jax.ShapeDtypeStruct((N_SG, b, SG_STATE), F32),
                   jax.ShapeDtypeStruct((N_SG, b, SG_STATE), F32)],
        compiler_params=_params("parallel"),
        name="ssm_sample",
    )(us, h0_re, h0_im, ws, wc0, dec)


def _gelu_tanh(x):
    return 0.5 * x * (1.0 + jnp.tanh(math.sqrt(2.0 / math.pi) * (x + 0.044715 * (x * x * x))))


def _merge_body(h_ref, o_ref, y_ref, us_ref, g_ref, wgate_ref, bgate_ref, d_ref, gluw_ref, glub_ref,
                wa_ref, ws_ref, wo_ref, out_ref):
    h = h_ref[...]
    u = _rms(h, g_ref[...]).astype(BF16)
    gates = jax.nn.sigmoid(_dot(u, wgate_ref[...]) + bgate_ref[...])
    y = _gelu_tanh(y_ref[...] + d_ref[...] * us_ref[...])
    ys = y * jax.nn.sigmoid(_dot(y.astype(BF16), gluw_ref[...]) + glub_ref[...])
    a = _dot(o_ref[...], wa_ref[...])
    s = _dot(ys.astype(BF16), ws_ref[...])
    mix = gates[:, :D_MODEL] * a + gates[:, D_MODEL:] * s
    out_ref[...] = h + _dot(mix.astype(BF16), wo_ref[...])


def _merge(h, o, y, us, layer, w, tm=512):
    m = h.shape[0]
    tm = min(tm, m)
    row = lambda n: pl.BlockSpec((tm, n), lambda i: (i, 0))
    return pl.pallas_call(
        _merge_body,
        grid=(m // tm,),
        in_specs=[row(D_MODEL), row(ATTN_WIDTH), row(SSM_WIDTH), row(SSM_WIDTH),
                  _const_spec((None, 1, D_MODEL), (layer, 0, 0)),
                  _const_spec((None, D_MODEL, 2 * D_MODEL), (layer, 0, 1)),
                  _const_spec((None, 1, 2 * D_MODEL), (layer, 0, 0)),
                  _const_spec((None, 1, SSM_WIDTH), (layer, 0, 0)),
                  _const_spec((None, SSM_WIDTH, SSM_WIDTH), (layer, 0, 0)),
                  _const_spec((None, 1, SSM_WIDTH), (layer, 0, 0)),
                  _const_spec((None, ATTN_WIDTH, D_MODEL), (layer, 0, 0)),
                  _const_spec((None, SSM_WIDTH, D_MODEL), (layer, 0, 0)),
                  _const_spec((None, D_MODEL, D_MODEL), (layer, 0, 0))],
        out_specs=row(D_MODEL),
        out_shape=jax.ShapeDtypeStruct((m, D_MODEL), F32),
        compiler_params=_params("parallel"),
        name="merge",
    )(h, o, y, us, w["norm_mix"], w["w_in"], w["b_gate"], w["ssm_d"], w["glu_w"], w["glu_b"],
      w["w_attn_out"], w["w_ssm_out"], w["w_out"])


def _rope_tables(pos):
    half = HEAD_DIM // 2
    inv = ROPE_THETA ** (-jnp.arange(half, dtype=F32) / half)
    ang = pos.astype(F32)[:, None] * inv[None, :]
    cos = jnp.cos(ang)
    sin = jnp.sin(ang)
    reps = ATTN_WIDTH // HEAD_DIM
    cos = jnp.tile(jnp.concatenate([cos, cos], axis=1), (1, reps))
    sin = jnp.tile(jnp.concatenate([-sin, sin], axis=1), (1, reps))
    return cos, sin, cos.T, sin.T


def kernel(x_prompt, x_sample, cache_k, cache_v, state_ssm_re, state_ssm_im, page_table, norm_ffn1, ffn1_w_gate, ffn1_w_up, ffn1_w_down, norm_mix, w_in, b_gate, lambda_q1, lambda_k1, lambda_q2, lambda_k2, attn_subln, w_attn_out, ssm_a_re, ssm_a_im, ssm_log_dt, ssm_b_re, ssm_b_im, ssm_c_re, ssm_c_im, ssm_d, glu_w, glu_b, w_ssm_out, w_out, norm_ffn2, ffn2_w_gate, ffn2_w_up, ffn2_w_down, final_norm):
    batch, seq, _ = x_prompt.shape
    dec_batch, dec_seq, _ = x_sample.shape
    depth = w_in.shape[0]
    n_pool = cache_k.shape[1]
    past_len = page_table.shape[1] * PAGE_SIZE
    assert dec_seq == 1

    vec = lambda x: x.reshape(depth, 1, -1)
    bf = lambda x: x.astype(BF16)
    w_in_b = bf(w_in)
    w_kt = bf(jnp.swapaxes(w_in[:, :, ATTN_WIDTH:2 * ATTN_WIDTH], 1, 2))
    mw = dict(norm_mix=vec(norm_mix), w_in=w_in_b, b_gate=vec(b_gate), ssm_d=vec(ssm_d), glu_w=bf(glu_w),
              glu_b=vec(glu_b), w_attn_out=bf(w_attn_out), w_ssm_out=bf(w_ssm_out), w_out=bf(w_out))
    f1 = (vec(norm_ffn1), bf(ffn1_w_gate), bf(ffn1_w_up), bf(ffn1_w_down))
    f2 = (vec(norm_ffn2), bf(ffn2_w_gate), bf(ffn2_w_up), bf(ffn2_w_down))
    lam_rows = jnp.stack([lambda_q1, lambda_k1, lambda_q2, lambda_k2], axis=1)
    subln = vec(attn_subln)
    final_g = final_norm.reshape(1, D_MODEL)

    tabs_p = _rope_tables(jnp.arange(seq, dtype=jnp.int32))
    tabs_s = _rope_tables(jnp.full((dec_batch,), past_len, jnp.int32))
    cache_kt = jnp.transpose(cache_k, (0, 1, 3, 4, 2)).reshape(depth, n_pool, ATTN_WIDTH, PAGE_SIZE)
    cache_v2 = cache_v.reshape(depth, n_pool, PAGE_SIZE * N_HEADS, 2 * HEAD_DIM)

    mats = _ssm_prep(ssm_a_re, ssm_a_im, ssm_log_dt, ssm_b_re, ssm_b_im, ssm_c_re, ssm_c_im)
    sg_state = lambda x: x.reshape(dec_batch, N_SG, SG_STATE).transpose(1, 0, 2)
    sg_unstate = lambda x: x.transpose(1, 0, 2).reshape(dec_batch, N_GROUPS, STATE_DIM)

    xp = x_prompt.reshape(batch * seq, D_MODEL)
    xs = x_sample.reshape(dec_batch, D_MODEL)
    outs = [[] for _ in range(8)]
    for l in range(depth):
        lam_init = 0.8 - 0.6 * math.exp(-0.3 * l)
        last = l == depth - 1

        h = _ffn(xp, l, *f1)
        q, kt, ktb, v4, vb, us = _inproj(h.reshape(batch, seq, D_MODEL), l, mw["norm_mix"], w_in_b, w_kt, tabs_p)
        o = _attn_prompt(q, ktb, vb, l, lam_rows, subln, lam_init)
        y, st = _ssm_prompt(us, l, mats)
        h = _merge(h, o.reshape(batch * seq, ATTN_WIDTH), y.reshape(batch * seq, SSM_WIDTH),
                   us.reshape(batch * seq, SSM_WIDTH), l, mw)
        xp = _ffn(h, l, *f2, final_g=final_g if last else None)
        outs[0].append(kt.reshape(batch, 2 * N_HEADS, HEAD_DIM, seq).transpose(0, 3, 1, 2))
        outs[1].append(v4.reshape(batch, seq, N_HEADS, 2 * HEAD_DIM))
        outs[2].append(st[:, :, 0, :].reshape(batch, N_GROUPS, STATE_DIM))
        outs[3].append(st[:, :, 1, :].reshape(batch, N_GROUPS, STATE_DIM))

        h = _ffn(xs, l, *f1)
        q, kt, _, v4, _, us, k_new = _inproj(h.reshape(1, dec_batch, D_MODEL), l, mw["norm_mix"], w_in_b, w_kt,
                                             tabs_s, k_rows=True)
        v_new4 = v4.reshape(dec_batch, N_HEADS, 2 * HEAD_DIM)
        o = _attn_decode(q.reshape(dec_batch, 1, ATTN_WIDTH), k_new.reshape(dec_batch, 1, ATTN_WIDTH), v_new4,
                         cache_kt, cache_v2, page_table, l, lam_rows, subln, lam_init)
        us = us.reshape(dec_batch, SSM_WIDTH)
        y, h_re, h_im = _ssm_sample(us, sg_state(state_ssm_re[l]), sg_state(state_ssm_im[l]), l, mats)
        h = _merge(h, o.reshape(dec_batch, ATTN_WIDTH), y, us, l, mw)
        xs = _ffn(h, l, *f2, final_g=final_g if last else None)
        outs[4].append(kt.reshape(2 * N_HEADS, HEAD_DIM, dec_batch).transpose(2, 0, 1)[:, None])
        outs[5].append(v_new4[:, None])
        outs[6].append(sg_unstate(h_re))
        outs[7].append(sg_unstate(h_im))

    stk = [jnp.stack(o) for o in outs]
    return (xp.reshape(batch, seq, D_MODEL), xs.reshape(dec_batch, dec_seq, D_MODEL),
            stk[0], stk[1], stk[2], stk[3], stk[4], stk[5], stk[6], stk[7])
```

```python
import functools
import math

import jax
import jax.numpy as jnp
from jax import lax
from jax.experimental import pallas as pl
from jax.experimental.pallas import tpu as pltpu

F32 = jnp.float32
BF16 = jnp.bfloat16

D_MODEL = 1024
N_HEADS = 4
HEAD_DIM = 64
ATTN_WIDTH = 2 * N_HEADS * HEAD_DIM
SSM_WIDTH = 512
GROUP_SIZE = 16
N_GROUPS = SSM_WIDTH // GROUP_SIZE
STATE_DIM = 64
D_FF = 2816
ROPE_THETA = 10000.0
NORM_EPS = 1e-6
PAGE_SIZE = 128

LANES = 128
SG_GROUPS = LANES // GROUP_SIZE
N_SG = N_GROUPS // SG_GROUPS
SG_STATE = SG_GROUPS * STATE_DIM
T_CHUNK = 8
N_DEC_ROWS = 16
FF_CHUNK = 256
VMEM_LIMIT = 56 * 2**20

_HI = lax.Precision.HIGHEST


def _dot(a, b):
    return jnp.dot(a, b, preferred_element_type=F32)


def _rms(x, g):
    return x * lax.rsqrt(jnp.mean(x * x, axis=-1, keepdims=True) + NORM_EPS) * g


def _tree(op, xs):
    xs = list(xs)
    while len(xs) > 1:
        xs = [op(xs[k], xs[k + 1]) if k + 1 < len(xs) else xs[k] for k in range(0, len(xs), 2)]
    return xs[0]


def _params(*sem):
    return pltpu.CompilerParams(dimension_semantics=sem, vmem_limit_bytes=VMEM_LIMIT)


def _const_spec(shape, index):
    return pl.BlockSpec(shape, lambda *_: index, pipeline_mode=pl.Buffered(1))


def _ffn_body(*refs, final):
    if final:
        x_ref, g_ref, wg_ref, wu_ref, wd_ref, fg_ref, o_ref = refs
    else:
        x_ref, g_ref, wg_ref, wu_ref, wd_ref, o_ref = refs
    x = x_ref[...]
    u = _rms(x, g_ref[...]).astype(BF16)
    acc = jnp.zeros(x.shape, F32)
    for c in range(D_FF // FF_CHUNK):
        cols = slice(c * FF_CHUNK, (c + 1) * FF_CHUNK)
        gate = _dot(u, wg_ref[:, cols])
        up = _dot(u, wu_ref[:, cols])
        act = (gate * jax.nn.sigmoid(gate) * up).astype(BF16)
        acc = acc + _dot(act, wd_ref[cols, :])
    y = x + 0.5 * acc
    if final:
        y = _rms(y, fg_ref[...])
    o_ref[...] = y


def _ffn(x, layer, g, wg, wu, wd, final_g=None, tm=512):
    m = x.shape[0]
    tm = min(tm, m)
    in_specs = [
        pl.BlockSpec((tm, D_MODEL), lambda i: (i, 0)),
        _const_spec((None, 1, D_MODEL), (layer, 0, 0)),
        _const_spec((None, D_MODEL, D_FF), (layer, 0, 0)),
        _const_spec((None, D_MODEL, D_FF), (layer, 0, 0)),
        _const_spec((None, D_FF, D_MODEL), (layer, 0, 0)),
    ]
    args = [x, g, wg, wu, wd]
    if final_g is not None:
        in_specs.append(_const_spec((1, D_MODEL), (0, 0)))
        args.append(final_g)
    return pl.pallas_call(
        functools.partial(_ffn_body, final=final_g is not None),
        grid=(m // tm,),
        in_specs=in_specs,
        out_specs=pl.BlockSpec((tm, D_MODEL), lambda i: (i, 0)),
        out_shape=jax.ShapeDtypeStruct((m, D_MODEL), F32),
        compiler_params=_params("parallel"),
        name="ffn",
    )(*args)


def _swap_halves_lanes(x):
    n = x.shape[-1]
    half = HEAD_DIM // 2
    lane = lax.broadcasted_iota(jnp.int32, x.shape, x.ndim - 1)
    return jnp.where(lane % HEAD_DIM < half, pltpu.roll(x, n - half, x.ndim - 1),
                     pltpu.roll(x, half, x.ndim - 1))


def _swap_halves_rows(x):
    half = HEAD_DIM // 2
    parts = []
    for r in range(0, x.shape[0], HEAD_DIM):
        parts += [x[r + half:r + HEAD_DIM], x[r:r + half]]
    return jnp.concatenate(parts, axis=0)


def _inproj_body(*refs, k_rows, n_aliased):
    n_in = 10 + (1 if k_rows else 0) + n_aliased
    x_ref, g_ref, wq_ref, wkt_ref, wv_ref, wus_ref, cos_ref, sin_ref, cost_ref, sint_ref = refs[:10]
    q_ref, kt_ref, ktb_ref, v4_ref, vb_ref, us_ref = refs[n_in:n_in + 6]
    tm = x_ref.shape[0]
    u = _rms(x_ref[...], g_ref[...]).astype(BF16)
    cos = cos_ref[...]
    sin = sin_ref[...]
    q = _dot(u, wq_ref[...])
    q = q * cos + _swap_halves_lanes(q) * sin
    q_ref[...] = (q * (HEAD_DIM ** -0.5)).astype(BF16)
    kt = lax.dot_general(wkt_ref[...], u, (((1,), (1,)), ((), ())), preferred_element_type=F32)
    kt = kt * cost_ref[...] + _swap_halves_rows(kt) * sint_ref[...]
    kt_ref[...] = kt
    ktb_ref[...] = kt.astype(BF16)
    v = _dot(u, wv_ref[...])
    vb_ref[...] = v.astype(BF16)
    for h in range(N_HEADS):
        v4_ref[pl.ds(h, tm, stride=N_HEADS), :] = v[:, h * LANES:(h + 1) * LANES]
    us_ref[...] = _dot(u, wus_ref[...])
    if k_rows:
        wk_ref, k_ref = refs[10], refs[n_in + 6]
        k = _dot(u, wk_ref[...])
        k_ref[...] = k * cos + _swap_halves_lanes(k) * sin


def _inproj(x, layer, g, w_in, w_kt, tabs, k_rows=False, stacked=None, tm=512):
    b, l, _ = x.shape
    tm = min(tm, l)
    cos, sin, cost, sint = tabs
    aw = ATTN_WIDTH
    wcol = lambda c: _const_spec((None, D_MODEL, aw), (layer, 0, c))
    in_specs = [
        pl.BlockSpec((None, tm, D_MODEL), lambda j, i: (i, j, 0)),
        _const_spec((None, 1, D_MODEL), (layer, 0, 0)),
        wcol(0),
        _const_spec((None, aw, D_MODEL), (layer, 0, 0)),
        wcol(2),
        wcol(3),
        pl.BlockSpec((tm, aw), lambda j, i: (j, 0)),
        pl.BlockSpec((tm, aw), lambda j, i: (j, 0)),
        pl.BlockSpec((aw, tm), lambda j, i: (0, j)),
        pl.BlockSpec((aw, tm), lambda j, i: (0, j)),
    ]
    args = [x, g, w_in, w_kt, w_in, w_in, cos, sin, cost, sint]
    row_spec = pl.BlockSpec((None, tm, aw), lambda j, i: (i, j, 0))
    col_spec = pl.BlockSpec((None, aw, tm), lambda j, i: (i, 0, j))
    out_specs = [row_spec, col_spec, col_spec,
                 pl.BlockSpec((None, tm * N_HEADS, LANES), lambda j, i: (i, j, 0)),
                 row_spec, row_spec]
    out_shape = [jax.ShapeDtypeStruct((b, l, aw), BF16),
                 jax.ShapeDtypeStruct((b, aw, l), F32),
                 jax.ShapeDtypeStruct((b, aw, l), BF16),
                 jax.ShapeDtypeStruct((b, l * N_HEADS, LANES), F32),
                 jax.ShapeDtypeStruct((b, l, aw), BF16),
                 jax.ShapeDtypeStruct((b, l, SSM_WIDTH), F32)]
    if k_rows:
        in_specs.append(wcol(1))
        args.append(w_in)
        out_specs.append(row_spec)
        out_shape.append(jax.ShapeDtypeStruct((b, l, aw), F32))
    aliases = {}
    if stacked is not None:
        for buf, out_idx in zip(stacked, (1, 3)):
            aliases[len(args)] = out_idx
            in_specs.append(pl.BlockSpec(memory_space=pl.ANY))
            args.append(buf)
            out_shape[out_idx] = jax.ShapeDtypeStruct(buf.shape, buf.dtype)
        out_specs[1] = pl.BlockSpec((None, None, aw, tm), lambda j, i: (layer, i, 0, j))
        out_specs[3] = pl.BlockSpec((None, None, tm * N_HEADS, LANES), lambda j, i: (layer, i, j, 0))
    return pl.pallas_call(
        functools.partial(_inproj_body, k_rows=k_rows, n_aliased=len(aliases)),
        grid=(l // tm, b),
        in_specs=in_specs,
        out_specs=out_specs,
        out_shape=out_shape,
        input_output_aliases=aliases,
        compiler_params=_params("parallel", "parallel"),
        name="in_proj",
    )(*args)


def _lambda(lam_ref, lam_init):
    a = jnp.sum(lam_ref[0:1, :] * lam_ref[1:2, :], axis=-1, keepdims=True)
    b = jnp.sum(lam_ref[2:3, :] * lam_ref[3:4, :], axis=-1, keepdims=True)
    return jnp.exp(a) - jnp.exp(b) + lam_init


def _head_norm(o, subln, lam_init):
    return _rms(o, subln) * (1.0 - lam_init)


def _attn_body(lam_ref, subln_ref, q_ref, kt_ref, v_ref, o_ref, s_scr, *, lam_init, tq):
    n_tiles = q_ref.shape[0] // tq
    lam = _lambda(lam_ref, lam_init)
    subln = subln_ref[...]
    lane = lax.broadcasted_iota(jnp.int32, (tq, LANES), 1)
    row = lax.broadcasted_iota(jnp.int32, (tq, tq), 0)
    col = lax.broadcasted_iota(jnp.int32, (tq, tq), 1)

    def one_map(i, which, qm):
        m_tile = jnp.full((tq, LANES), -jnp.inf, F32)
        for j in range(i + 1):
            s = _dot(qm, kt_ref[:, j * tq:(j + 1) * tq])
            if j == i:
                s = jnp.where(col <= row, s, -jnp.inf)
            s_scr[which, :, j * tq:(j + 1) * tq] = s
            for c in range(tq // LANES):
                m_tile = jnp.maximum(m_tile, s[:, c * LANES:(c + 1) * LANES])
        m_b = jnp.broadcast_to(jnp.max(m_tile, axis=-1, keepdims=True), (tq, LANES))
        l_tile = jnp.zeros((tq, LANES), F32)
        acc = jnp.zeros((tq, LANES), F32)
        for j in range(i + 1):
            parts = []
            for c in range(tq // LANES):
                p = jnp.exp(s_scr[which, :, j * tq + c * LANES:j * tq + (c + 1) * LANES] - m_b)
                l_tile = l_tile + p
                parts.append(p.astype(BF16))
            acc = acc + _dot(jnp.concatenate(parts, axis=1), v_ref[j * tq:(j + 1) * tq, :])
        return acc / jnp.sum(l_tile, axis=-1, keepdims=True)

    for i in range(n_tiles):
        rows = slice(i * tq, (i + 1) * tq)
        q = q_ref[rows, :]
        zero = jnp.zeros_like(q)
        o1 = one_map(i, 0, jnp.where(lane < HEAD_DIM, q, zero))
        o2 = one_map(i, 1, jnp.where(lane >= HEAD_DIM, q, zero))
        o_ref[rows, :] = _head_norm(o1 - lam * o2, subln, lam_init).astype(o_ref.dtype)


def _attn_prompt(q, ktb, vb, layer, lam_rows, subln, lam_init, tq=256):
    b, l, _ = q.shape
    rows = pl.BlockSpec((None, l, LANES), lambda i, h: (i, 0, h))
    return pl.pallas_call(
        functools.partial(_attn_body, lam_init=lam_init, tq=tq),
        grid=(b, N_HEADS),
        in_specs=[
            _const_spec((None, 4, HEAD_DIM), (layer, 0, 0)),
            _const_spec((None, 1, LANES), (layer, 0, 0)),
            rows,
            pl.BlockSpec((None, LANES, l), lambda i, h: (i, h, 0)),
            rows,
        ],
        out_specs=rows,
        out_shape=jax.ShapeDtypeStruct((b, l, ATTN_WIDTH), BF16),
        scratch_shapes=[pltpu.VMEM((2, tq, l), F32)],
        compiler_params=_params("parallel", "parallel"),
        name="attn_prompt",
    )(lam_rows, subln, q, ktb, vb)


def _decode_body(pt_ref, lam_ref, subln_ref, q_ref, knew_ref, vnew_ref, *refs, lam_init, n_pages):
    del pt_ref
    k_refs = refs[:n_pages]
    v_refs = refs[n_pages:2 * n_pages]
    o_ref = refs[2 * n_pages]
    n_sub = 2 * N_HEADS
    row = lax.broadcasted_iota(jnp.int32, (n_sub, ATTN_WIDTH), 0)
    lane = lax.broadcasted_iota(jnp.int32, (n_sub, ATTN_WIDTH), 1)
    sub = jnp.where(row < N_HEADS, 2 * row, 2 * (row - N_HEADS) + 1)
    q_blk = jnp.where(lane // HEAD_DIM == sub, q_ref[...].astype(F32), 0.0)
    q_bf = q_blk.astype(BF16)
    s_new = jnp.sum(q_blk * knew_ref[...], axis=-1, keepdims=True)
    scores = [_dot(q_bf, k_refs[p][...].astype(BF16)) for p in range(n_pages)]
    m = jnp.maximum(s_new, jnp.max(_tree(jnp.maximum, scores), axis=-1, keepdims=True))
    p_new = jnp.exp(s_new - m)
    probs = [jnp.exp(s - m) for s in scores]
    l = p_new + jnp.sum(_tree(jnp.add, probs), axis=-1, keepdims=True)
    lam = _lambda(lam_ref, lam_init)
    inv = 1.0 / l
    r8 = lax.broadcasted_iota(jnp.int32, (n_sub, 1), 0)
    coef = jnp.where(r8 < N_HEADS, inv, -lam * inv)

    def fold(x):
        y = x * coef
        y = y + pltpu.roll(y, N_HEADS, 0)
        return jnp.where(lax.broadcasted_iota(jnp.int32, y.shape, 0) < N_HEADS, y, 0.0)

    w_new = fold(jnp.broadcast_to(p_new, (n_sub, LANES)))
    vnew = vnew_ref[...]
    vnew8 = jnp.concatenate([vnew, jnp.zeros_like(vnew)], axis=0)
    terms = [w_new * vnew8]
    rsel = lax.broadcasted_iota(jnp.int32, (n_sub, PAGE_SIZE), 0)
    for p in range(n_pages):
        w = fold(probs[p])
        for h in range(N_HEADS):
            wh = jnp.where(rsel == h, w, 0.0).astype(BF16)
            vh = v_refs[p][pl.ds(h, PAGE_SIZE, stride=N_HEADS), :].astype(BF16)
            terms.append(_dot(wh, vh))
    acc = _tree(jnp.add, terms)
    o = _head_norm(acc[:N_HEADS], subln_ref[...], lam_init)
    o_ref[...] = o.astype(o_ref.dtype)


def _attn_decode(q, k_new, v_new4, cache_kt, cache_v, page_table, layer, lam_rows, subln, lam_init):
    b = q.shape[0]
    n_pages = page_table.shape[1]
    page_spec = lambda p: pl.BlockSpec((None, None, ATTN_WIDTH, LANES),
                                       lambda i, pt: (layer, pt[i, p], 0, 0))
    in_specs = [
        pl.BlockSpec((None, 4, HEAD_DIM), lambda i, pt: (layer, 0, 0)),
        pl.BlockSpec((None, 1, LANES), lambda i, pt: (layer, 0, 0)),
        pl.BlockSpec((None, 1, ATTN_WIDTH), lambda i, pt: (i, 0, 0)),
        pl.BlockSpec((None, 1, ATTN_WIDTH), lambda i, pt: (i, 0, 0)),
        pl.BlockSpec((None, N_HEADS, LANES), lambda i, pt: (i, 0, 0)),
    ] + [page_spec(p) for p in range(n_pages)] * 2
    grid_spec = pltpu.PrefetchScalarGridSpec(
        num_scalar_prefetch=1,
        grid=(b,),
        in_specs=in_specs,
        out_specs=pl.BlockSpec((None, N_HEADS, LANES), lambda i, pt: (i, 0, 0)),
    )
    return pl.pallas_call(
        functools.partial(_decode_body, lam_init=lam_init, n_pages=n_pages),
        grid_spec=grid_spec,
        out_shape=jax.ShapeDtypeStruct((b, N_HEADS, LANES), BF16),
        compiler_params=_params("arbitrary"),
        name="attn_decode",
    )(page_table, lam_rows, subln, q, k_new, v_new4,
      *([cache_kt] * n_pages), *([cache_v] * n_pages))


def _cmul(ar, ai, br, bi):
    return ar * br - ai * bi, ar * bi + ai * br


def _ssm_prep_body(are_ref, aim_ref, ldt_ref, btr_ref, bti_ref, ctr_ref, cti_ref,
                   wk_ref, ws_ref, wc_ref, wc0_ref, dec_ref):
    t = T_CHUNK
    a_re = are_ref[...]
    a_im = aim_ref[...]
    dt = jnp.exp(ldt_ref[...])
    mag = jnp.exp(a_re * dt)
    p_re = mag * jnp.cos(a_im * dt)
    p_im = mag * jnp.sin(a_im * dt)
    den = a_re * a_re + a_im * a_im
    nr = p_re - 1.0
    coef_re = (nr * a_re + p_im * a_im) / den
    coef_im = (p_im * a_re - nr * a_im) / den

    r = lax.broadcasted_iota(jnp.int32, (LANES, SG_STATE), 0)
    c = lax.broadcasted_iota(jnp.int32, (LANES, SG_STATE), 1)
    bt_mask = r // GROUP_SIZE == c // STATE_DIM
    r = lax.broadcasted_iota(jnp.int32, (SG_STATE, LANES), 0)
    c = lax.broadcasted_iota(jnp.int32, (SG_STATE, LANES), 1)
    ct_mask = r // STATE_DIM == c // GROUP_SIZE
    btr = jnp.where(bt_mask, btr_ref[...], 0.0)
    bti = jnp.where(bt_mask, bti_ref[...], 0.0)
    ctr = jnp.where(ct_mask, ctr_ref[...], 0.0)
    cti = jnp.where(ct_mask, cti_ref[...], 0.0)
    bbr, bbi = _cmul(coef_re, coef_im, btr, bti)

    pow_re = [jnp.ones_like(p_re)]
    pow_im = [jnp.zeros_like(p_im)]
    col = lambda x: jnp.transpose(jnp.broadcast_to(x, (LANES, SG_STATE)))
    p_re_c, p_im_c = col(p_re), col(p_im)
    cpow_re = [jnp.ones_like(p_re_c)]
    cpow_im = [jnp.zeros_like(p_im_c)]
    for _ in range(t):
        nr_, ni_ = _cmul(pow_re[-1], pow_im[-1], p_re, p_im)
        pow_re.append(nr_)
        pow_im.append(ni_)
        nr_, ni_ = _cmul(cpow_re[-1], cpow_im[-1], p_re_c, p_im_c)
        cpow_re.append(nr_)
        cpow_im.append(ni_)

    ct_cat = jnp.concatenate([ctr, cti], axis=0)
    wk_ref[...] = jnp.zeros(wk_ref.shape, wk_ref.dtype)
    for j in range(t):
        er, ei = _cmul(bbr, bbi, pow_re[j], pow_im[j])
        kj = jnp.dot(jnp.concatenate([er, -ei], axis=1), ct_cat, precision=_HI,
                     preferred_element_type=F32).astype(BF16)
        for t_in in range(t - j):
            wk_ref[t_in * LANES:(t_in + 1) * LANES, (t_in + j) * LANES:(t_in + j + 1) * LANES] = kj
        rows = slice((t - 1 - j) * LANES, (t - j) * LANES)
        ws_ref[rows, :SG_STATE] = er.astype(BF16)
        ws_ref[rows, SG_STATE:] = ei.astype(BF16)
    for j in range(1, t + 1):
        cols = slice((j - 1) * LANES, j * LANES)
        wc_ref[:SG_STATE, cols] = (ctr * cpow_re[j] - cti * cpow_im[j]).astype(BF16)
        wc_ref[SG_STATE:, cols] = (-(ctr * cpow_im[j] + cti * cpow_re[j])).astype(BF16)
    wc0_ref[:SG_STATE, :] = ctr.astype(BF16)
    wc0_ref[SG_STATE:, :] = (-cti).astype(BF16)

    dec_rows = [jnp.concatenate([p_re, p_im], axis=1)]
    dr, di = pow_re[t], pow_im[t]
    for _ in range(N_DEC_ROWS - 1):
        dec_rows.append(jnp.concatenate([dr, di], axis=1))
        dr, di = _cmul(dr, di, dr, di)
    dec_ref[...] = jnp.concatenate(dec_rows, axis=0)


def _ssm_prep(a_re, a_im, log_dt, b_re, b_im, c_re, c_im):
    depth = a_re.shape[0]
    rowed = lambda x: x.reshape(depth, N_SG, 1, SG_STATE)
    ldt = jnp.broadcast_to(log_dt[:, :, None], (depth, N_GROUPS, STATE_DIM))

    def bt(x):
        y = x.reshape(depth, N_SG, SG_GROUPS, STATE_DIM, GROUP_SIZE).transpose(0, 1, 4, 2, 3)
        y = y.reshape(depth, N_SG, 1, GROUP_SIZE, SG_STATE)
        return jnp.broadcast_to(y, (depth, N_SG, SG_GROUPS, GROUP_SIZE, SG_STATE)).reshape(
            depth, N_SG, LANES, SG_STATE)

    def ct(x):
        y = x.reshape(depth, N_SG, SG_GROUPS, GROUP_SIZE, STATE_DIM).transpose(0, 1, 4, 2, 3)
        y = y.reshape(depth, N_SG, 1, STATE_DIM, LANES)
        return jnp.broadcast_to(y, (depth, N_SG, SG_GROUPS, STATE_DIM, LANES)).reshape(
            depth, N_SG, SG_STATE, LANES)

    tk = T_CHUNK * LANES
    blk = lambda *shape: pl.BlockSpec((None, None) + shape, lambda l, g: (l, g, 0, 0))
    return pl.pallas_call(
        _ssm_prep_body,
        grid=(depth, N_SG),
        in_specs=[blk(1, SG_STATE)] * 3 + [blk(LANES, SG_STATE)] * 2 + [blk(SG_STATE, LANES)] * 2,
        out_specs=[blk(tk, tk), blk(tk, 2 * SG_STATE), blk(2 * SG_STATE, tk),
                   blk(2 * SG_STATE, LANES), blk(N_DEC_ROWS, 2 * SG_STATE)],
        out_shape=[jax.ShapeDtypeStruct((depth, N_SG, tk, tk), BF16),
                   jax.ShapeDtypeStruct((depth, N_SG, tk, 2 * SG_STATE), BF16),
                   jax.ShapeDtypeStruct((depth, N_SG, 2 * SG_STATE, tk), BF16),
                   jax.ShapeDtypeStruct((depth, N_SG, 2 * SG_STATE, LANES), BF16),
                   jax.ShapeDtypeStruct((depth, N_SG, N_DEC_ROWS, 2 * SG_STATE), F32)],
        compiler_params=_params("parallel", "parallel"),
        name="ssm_prep",
    )(rowed(a_re), rowed(a_im), rowed(ldt), bt(b_re), bt(b_im), ct(c_re), ct(c_im))


def _ssm_prompt_body(us_ref, wk_ref, ws_ref, wc_ref, dec_ref, y_ref, st_ref):
    t = T_CHUNK
    n_chunks = us_ref.shape[0] // t
    x = jnp.concatenate([us_ref[pl.ds(i, n_chunks, stride=t), :] for i in range(t)], axis=1).astype(BF16)
    h_loc = _dot(x, ws_ref[...])
    s_re = h_loc[:, :SG_STATE]
    s_im = h_loc[:, SG_STATE:]
    row = lax.broadcasted_iota(jnp.int32, s_re.shape, 0)
    shifted = lambda v, d: jnp.where(row >= d, pltpu.roll(v, d, 0), 0.0)
    d = 1
    i = 1
    while d < n_chunks:
        a_r = dec_ref[i:i + 1, :SG_STATE]
        a_i = dec_ref[i:i + 1, SG_STATE:]
        add_re, add_im = _cmul(a_r, a_i, shifted(s_re, d), shifted(s_im, d))
        s_re = s_re + add_re
        s_im = s_im + add_im
        d *= 2
        i += 1
    st_ref[0:1, :] = s_re[n_chunks - 1:n_chunks, :]
    st_ref[1:2, :] = s_im[n_chunks - 1:n_chunks, :]
    h_in = jnp.concatenate([shifted(s_re, 1), shifted(s_im, 1)], axis=1).astype(BF16)
    y = _dot(x, wk_ref[...]) + _dot(h_in, wc_ref[...])
    for i in range(t):
        y_ref[pl.ds(i, n_chunks, stride=t), :] = y[:, i * LANES:(i + 1) * LANES]


def _ssm_prompt(us, layer, mats):
    wk, ws, wc, _, dec = mats
    b, l, _ = us.shape
    assert l // T_CHUNK <= 2 ** (N_DEC_ROWS - 1)
    tk = T_CHUNK * LANES
    mat = lambda r, c: pl.BlockSpec((None, None, r, c), lambda g, i: (layer, g, 0, 0))
    return pl.pallas_call(
        _ssm_prompt_body,
        grid=(N_SG, b),
        in_specs=[pl.BlockSpec((None, l, LANES), lambda g, i: (i, 0, g)),
                  mat(tk, tk), mat(tk, 2 * SG_STATE), mat(2 * SG_STATE, tk),
                  mat(N_DEC_ROWS, 2 * SG_STATE)],
        out_specs=[pl.BlockSpec((None, l, LANES), lambda g, i: (i, 0, g)),
                   pl.BlockSpec((None, None, 2, SG_STATE), lambda g, i: (i, g, 0, 0))],
        out_shape=[jax.ShapeDtypeStruct((b, l, SSM_WIDTH), F32),
                   jax.ShapeDtypeStruct((b, N_SG, 2, SG_STATE), F32)],
        compiler_params=_params("parallel", "parallel"),
        name="ssm_prompt",
    )(us, wk, ws, wc, dec)


def _ssm_sample_body(us_ref, h0r_ref, h0i_ref, ws_ref, wc0_ref, dec_ref, y_ref, hr_ref, hi_ref):
    bu = _dot(us_ref[...].astype(BF16), ws_ref[...])
    a_r = dec_ref[0:1, :SG_STATE]
    a_i = dec_ref[0:1, SG_STATE:]
    dr, di = _cmul(a_r, a_i, h0r_ref[...], h0i_ref[...])
    h_re = bu[:, :SG_STATE] + dr
    h_im = bu[:, SG_STATE:] + di
    hr_ref[...] = h_re
    hi_ref[...] = h_im
    y_ref[...] = _dot(jnp.concatenate([h_re, h_im], axis=1).astype(BF16), wc0_ref[...])


def _ssm_sample(us, h0_re, h0_im, layer, mats):
    _, ws, _, wc0, dec = mats
    b = us.shape[0]
    st_spec = pl.BlockSpec((None, b, SG_STATE), lambda g: (g, 0, 0))
    return pl.pallas_call(
        _ssm_sample_body,
        grid=(N_SG,),
        in_specs=[pl.BlockSpec((b, LANES), lambda g: (0, g)), st_spec, st_spec,
                  pl.BlockSpec((None, None, LANES, 2 * SG_STATE), lambda g: (layer, g, T_CHUNK - 1, 0)),
                  pl.BlockSpec((None, None, 2 * SG_STATE, LANES), lambda g: (layer, g, 0, 0)),
                  pl.BlockSpec((None, None, N_DEC_ROWS, 2 * SG_STATE), lambda g: (layer, g, 0, 0))],
        out_specs=[pl.BlockSpec((b, LANES), lambda g: (0, g)), st_spec, st_spec],
        out_shape=[jax.ShapeDtypeStruct((b, SSM_WIDTH), F32),
                   jax.ShapeDtypeStruct((N_SG, b, SG_STATE), F32),
                   jax.ShapeDtypeStruct((N_SG, b, SG_STATE), F32)],
        compiler_params=_params("parallel"),
        name="ssm_sample",
    )(us, h0_re, h0_im, ws, wc0, dec)


def _gelu_tanh(x):
    return 0.5 * x * (1.0 + jnp.tanh(math.sqrt(2.0 / math.pi) * (x + 0.044715 * (x * x * x))))


def _merge_body(h_ref, o_ref, y_ref, us_ref, g_ref, wgate_ref, bgate_ref, d_ref, gluw_ref, glub_ref,
                wa_ref, ws_ref, wo_ref, out_ref):
    h = h_ref[...]
    u = _rms(h, g_ref[...]).astype(BF16)
    gates = jax.nn.sigmoid(_dot(u, wgate_ref[...]) + bgate_ref[...])
    y = _gelu_tanh(y_ref[...] + d_ref[...] * us_ref[...])
    ys = y * jax.nn.sigmoid(_dot(y.astype(BF16), gluw_ref[...]) + glub_ref[...])
    a = _dot(o_ref[...], wa_ref[...])
    s = _dot(ys.astype(BF16), ws_ref[...])
    mix = gates[:, :D_MODEL] * a + gates[:, D_MODEL:] * s
    out_ref[...] = h + _dot(mix.astype(BF16), wo_ref[...])


def _merge(h, o, y, us, layer, w, tm=512):
    m = h.shape[0]
    tm = min(tm, m)
    row = lambda n: pl.BlockSpec((tm, n), lambda i: (i, 0))
    return pl.pallas_call(
        _merge_body,
        grid=(m // tm,),
        in_specs=[row(D_MODEL), row(ATTN_WIDTH), row(SSM_WIDTH), row(SSM_WIDTH),
                  _const_spec((None, 1, D_MODEL), (layer, 0, 0)),
                  _const_spec((None, D_MODEL, 2 * D_MODEL), (layer, 0, 1)),
                  _const_spec((None, 1, 2 * D_MODEL), (layer, 0, 0)),
                  _const_spec((None, 1, SSM_WIDTH), (layer, 0, 0)),
                  _const_spec((None, SSM_WIDTH, SSM_WIDTH), (layer, 0, 0)),
                  _const_spec((None, 1, SSM_WIDTH), (layer, 0, 0)),
                  _const_spec((None, ATTN_WIDTH, D_MODEL), (layer, 0, 0)),
                  _const_spec((None, SSM_WIDTH, D_MODEL), (layer, 0, 0)),
                  _const_spec((None, D_MODEL, D_MODEL), (layer, 0, 0))],
        out_specs=row(D_MODEL),
        out_shape=jax.ShapeDtypeStruct((m, D_MODEL), F32),
        compiler_params=_params("parallel"),
        name="merge",
    )(h, o, y, us, w["norm_mix"], w["w_in"], w["b_gate"], w["ssm_d"], w["glu_w"], w["glu_b"],
      w["w_attn_out"], w["w_ssm_out"], w["w_out"])


def _rope_tables(pos):
    half = HEAD_DIM // 2
    inv = ROPE_THETA ** (-jnp.arange(half, dtype=F32) / half)
    ang = pos.astype(F32)[:, None] * inv[None, :]
    cos = jnp.cos(ang)
    sin = jnp.sin(ang)
    reps = ATTN_WIDTH // HEAD_DIM
    cos = jnp.tile(jnp.concatenate([cos, cos], axis=1), (1, reps))
    sin = jnp.tile(jnp.concatenate([-sin, sin], axis=1), (1, reps))
    return cos, sin, cos.T, sin.T


def kernel(x_prompt, x_sample, cache_k, cache_v, state_ssm_re, state_ssm_im, page_table, norm_ffn1, ffn1_w_gate, ffn1_w_up, ffn1_w_down, norm_mix, w_in, b_gate, lambda_q1, lambda_k1, lambda_q2, lambda_k2, attn_subln, w_attn_out, ssm_a_re, ssm_a_im, ssm_log_dt, ssm_b_re, ssm_b_im, ssm_c_re, ssm_c_im, ssm_d, glu_w, glu_b, w_ssm_out, w_out, norm_ffn2, ffn2_w_gate, ffn2_w_up, ffn2_w_down, final_norm):
    batch, seq, _ = x_prompt.shape
    dec_batch, dec_seq, _ = x_sample.shape
    depth = w_in.shape[0]
    n_pool = cache_k.shape[1]
    past_len = page_table.shape[1] * PAGE_SIZE
    assert dec_seq == 1

    vec = lambda x: x.reshape(depth, 1, -1)
    bf = lambda x: x.astype(BF16)
    w_in_b = bf(w_in)
    w_kt = bf(jnp.swapaxes(w_in[:, :, ATTN_WIDTH:2 * ATTN_WIDTH], 1, 2))
    mw = dict(norm_mix=vec(norm_mix), w_in=w_in_b, b_gate=vec(b_gate), ssm_d=vec(ssm_d), glu_w=bf(glu_w),
              glu_b=vec(glu_b), w_attn_out=bf(w_attn_out), w_ssm_out=bf(w_ssm_out), w_out=bf(w_out))
    f1 = (vec(norm_ffn1), bf(ffn1_w_gate), bf(ffn1_w_up), bf(ffn1_w_down))
    f2 = (vec(norm_ffn2), bf(ffn2_w_gate), bf(ffn2_w_up), bf(ffn2_w_down))
    lam_rows = jnp.stack([lambda_q1, lambda_k1, lambda_q2, lambda_k2], axis=1)
    subln = vec(attn_subln)
    final_g = final_norm.reshape(1, D_MODEL)

    tabs_p = _rope_tables(jnp.arange(seq, dtype=jnp.int32))
    tabs_s = _rope_tables(jnp.full((dec_batch,), past_len, jnp.int32))
    cache_kt = jnp.transpose(cache_k, (0, 1, 3, 4, 2)).reshape(depth, n_pool, ATTN_WIDTH, PAGE_SIZE)
    cache_v2 = cache_v.reshape(depth, n_pool, PAGE_SIZE * N_HEADS, 2 * HEAD_DIM)

    mats = _ssm_prep(ssm_a_re, ssm_a_im, ssm_log_dt, ssm_b_re, ssm_b_im, ssm_c_re, ssm_c_im)
    sg_state = lambda x: x.reshape(dec_batch, N_SG, SG_STATE).transpose(1, 0, 2)
    sg_unstate = lambda x: x.transpose(1, 0, 2).reshape(dec_batch, N_GROUPS, STATE_DIM)

    xp = x_prompt.reshape(batch * seq, D_MODEL)
    xs = x_sample.reshape(dec_batch, D_MODEL)
    outs = [[] for _ in range(8)]
    kt_all = jnp.zeros((depth, batch, ATTN_WIDTH, seq), F32)
    v4_all = jnp.zeros((depth, batch, seq * N_HEADS, 2 * HEAD_DIM), F32)
    for l in range(depth):
        lam_init = 0.8 - 0.6 * math.exp(-0.3 * l)
        last = l == depth - 1

        h = _ffn(xp, l, *f1)
        q, kt_all, ktb, v4_all, vb, us = _inproj(h.reshape(batch, seq, D_MODEL), l, mw["norm_mix"], w_in_b, w_kt,
                                                 tabs_p, stacked=(kt_all, v4_all))
        o = _attn_prompt(q, ktb, vb, l, lam_rows, subln, lam_init)
        y, st = _ssm_prompt(us, l, mats)
        h = _merge(h, o.reshape(batch * seq, ATTN_WIDTH), y.reshape(batch * seq, SSM_WIDTH),
                   us.reshape(batch * seq, SSM_WIDTH), l, mw)
        xp = _ffn(h, l, *f2, final_g=final_g if last else None)
        outs[2].append(st[:, :, 0, :].reshape(batch, N_GROUPS, STATE_DIM))
        outs[3].append(st[:, :, 1, :].reshape(batch, N_GROUPS, STATE_DIM))

        h = _ffn(xs, l, *f1)
        q, kt, _, v4, _, us, k_new = _inproj(h.reshape(1, dec_batch, D_MODEL), l, mw["norm_mix"], w_in_b, w_kt,
                                             tabs_s, k_rows=True)
        v_new4 = v4.reshape(dec_batch, N_HEADS, 2 * HEAD_DIM)
        o = _attn_decode(q.reshape(dec_batch, 1, ATTN_WIDTH), k_new.reshape(dec_batch, 1, ATTN_WIDTH), v_new4,
                         cache_kt, cache_v2, page_table, l, lam_rows, subln, lam_init)
        us = us.reshape(dec_batch, SSM_WIDTH)
        y, h_re, h_im = _ssm_sample(us, sg_state(state_ssm_re[l]), sg_state(state_ssm_im[l]), l, mats)
        h = _merge(h, o.reshape(dec_batch, ATTN_WIDTH), y, us, l, mw)
        xs = _ffn(h, l, *f2, final_g=final_g if last else None)
        outs[4].append(kt.reshape(2 * N_HEADS, HEAD_DIM, dec_batch).transpose(2, 0, 1)[:, None])
        outs[5].append(v_new4[:, None])
        outs[6].append(sg_unstate(h_re))
        outs[7].append(sg_unstate(h_im))

    stk = [jnp.stack(o) for o in outs[2:]]
    k_prompt = kt_all.reshape(depth, batch, 2 * N_HEADS, HEAD_DIM, seq).transpose(0, 1, 4, 2, 3)
    v_prompt = v4_all.reshape(depth, batch, seq, N_HEADS, 2 * HEAD_DIM)
    return (xp.reshape(batch, seq, D_MODEL), xs.reshape(dec_batch, dec_seq, D_MODEL),
            k_prompt, v_prompt, stk[0], stk[1], stk[2], stk[3], stk[4], stk[5])
```

```python
import functools
import math

import jax
import jax.numpy as jnp
from jax import lax
from jax.experimental import pallas as pl
from jax.experimental.pallas import tpu as pltpu

F32 = jnp.float32
BF16 = jnp.bfloat16

D_MODEL = 1024
N_HEADS = 4
HEAD_DIM = 64
ATTN_WIDTH = 2 * N_HEADS * HEAD_DIM
SSM_WIDTH = 512
GROUP_SIZE = 16
N_GROUPS = SSM_WIDTH // GROUP_SIZE
STATE_DIM = 64
D_FF = 2816
ROPE_THETA = 10000.0
NORM_EPS = 1e-6
PAGE_SIZE = 128

LANES = 128
SG_GROUPS = LANES // GROUP_SIZE
N_SG = N_GROUPS // SG_GROUPS
SG_STATE = SG_GROUPS * STATE_DIM
T_CHUNK = 8
N_DEC_ROWS = 16
FF_CHUNK = 256
VMEM_LIMIT = 56 * 2**20

_HI = lax.Precision.HIGHEST


def _dot(a, b):
    return jnp.dot(a, b, preferred_element_type=F32)


def _rms(x, g):
    return x * lax.rsqrt(jnp.mean(x * x, axis=-1, keepdims=True) + NORM_EPS) * g


def _tree(op, xs):
    xs = list(xs)
    while len(xs) > 1:
        xs = [op(xs[k], xs[k + 1]) if k + 1 < len(xs) else xs[k] for k in range(0, len(xs), 2)]
    return xs[0]


def _params(*sem):
    return pltpu.CompilerParams(dimension_semantics=sem, vmem_limit_bytes=VMEM_LIMIT)


def _const_spec(shape, index):
    return pl.BlockSpec(shape, lambda *_: index, pipeline_mode=pl.Buffered(1))


def _ffn_body(*refs, final):
    if final:
        x_ref, g_ref, wg_ref, wu_ref, wd_ref, fg_ref, o_ref = refs
    else:
        x_ref, g_ref, wg_ref, wu_ref, wd_ref, o_ref = refs
    x = x_ref[...]
    u = _rms(x, g_ref[...]).astype(BF16)
    acc = jnp.zeros(x.shape, F32)
    for c in range(D_FF // FF_CHUNK):
        cols = slice(c * FF_CHUNK, (c + 1) * FF_CHUNK)
        gate = _dot(u, wg_ref[:, cols])
        up = _dot(u, wu_ref[:, cols])
        act = (gate * jax.nn.sigmoid(gate) * up).astype(BF16)
        acc = acc + _dot(act, wd_ref[cols, :])
    y = x + 0.5 * acc
    if final:
        y = _rms(y, fg_ref[...])
    o_ref[...] = y


def _ffn(x, layer, g, wg, wu, wd, final_g=None, tm=512):
    m = x.shape[0]
    tm = min(tm, m)
    in_specs = [
        pl.BlockSpec((tm, D_MODEL), lambda i: (i, 0)),
        _const_spec((None, 1, D_MODEL), (layer, 0, 0)),
        _const_spec((None, D_MODEL, D_FF), (layer, 0, 0)),
        _const_spec((None, D_MODEL, D_FF), (layer, 0, 0)),
        _const_spec((None, D_FF, D_MODEL), (layer, 0, 0)),
    ]
    args = [x, g, wg, wu, wd]
    if final_g is not None:
        in_specs.append(_const_spec((1, D_MODEL), (0, 0)))
        args.append(final_g)
    return pl.pallas_call(
        functools.partial(_ffn_body, final=final_g is not None),
        grid=(m // tm,),
        in_specs=in_specs,
        out_specs=pl.BlockSpec((tm, D_MODEL), lambda i: (i, 0)),
        out_shape=jax.ShapeDtypeStruct((m, D_MODEL), F32),
        compiler_params=_params("parallel"),
        name="ffn",
    )(*args)


def _swap_halves_lanes(x):
    n = x.shape[-1]
    half = HEAD_DIM // 2
    lane = lax.broadcasted_iota(jnp.int32, x.shape, x.ndim - 1)
    return jnp.where(lane % HEAD_DIM < half, pltpu.roll(x, n - half, x.ndim - 1),
                     pltpu.roll(x, half, x.ndim - 1))


def _swap_halves_rows(x):
    half = HEAD_DIM // 2
    parts = []
    for r in range(0, x.shape[0], HEAD_DIM):
        parts += [x[r + half:r + HEAD_DIM], x[r:r + half]]
    return jnp.concatenate(parts, axis=0)


def _inproj_body(*refs, k_rows, n_aliased):
    n_in = 10 + (1 if k_rows else 0) + n_aliased
    x_ref, g_ref, wq_ref, wkt_ref, wv_ref, wus_ref, cos_ref, sin_ref, cost_ref, sint_ref = refs[:10]
    q_ref, kt_ref, ktb_ref, v4_ref, vb_ref, us_ref = refs[n_in:n_in + 6]
    tm = x_ref.shape[0]
    u = _rms(x_ref[...], g_ref[...]).astype(BF16)
    cos = cos_ref[...]
    sin = sin_ref[...]
    q = _dot(u, wq_ref[...])
    q = q * cos + _swap_halves_lanes(q) * sin
    q_ref[...] = (q * (HEAD_DIM ** -0.5)).astype(BF16)
    kt = lax.dot_general(wkt_ref[...], u, (((1,), (1,)), ((), ())), preferred_element_type=F32)
    kt = kt * cost_ref[...] + _swap_halves_rows(kt) * sint_ref[...]
    kt_ref[...] = kt
    ktb_ref[...] = kt.astype(BF16)
    v = _dot(u, wv_ref[...])
    vb_ref[...] = v.astype(BF16)
    for h in range(N_HEADS):
        v4_ref[pl.ds(h, tm, stride=N_HEADS), :] = v[:, h * LANES:(h + 1) * LANES]
    us_ref[...] = _dot(u, wus_ref[...])
    if k_rows:
        wk_ref, k_ref = refs[10], refs[n_in + 6]
        k = _dot(u, wk_ref[...])
        k_ref[...] = k * cos + _swap_halves_lanes(k) * sin


def _inproj(x, layer, g, w_in, w_kt, tabs, k_rows=False, stacked=None, tm=512):
    b, l, _ = x.shape
    tm = min(tm, l)
    cos, sin, cost, sint = tabs
    aw = ATTN_WIDTH
    wcol = lambda c: _const_spec((None, D_MODEL, aw), (layer, 0, c))
    in_specs = [
        pl.BlockSpec((None, tm, D_MODEL), lambda j, i: (i, j, 0)),
        _const_spec((None, 1, D_MODEL), (layer, 0, 0)),
        wcol(0),
        _const_spec((None, aw, D_MODEL), (layer, 0, 0)),
        wcol(2),
        wcol(3),
        pl.BlockSpec((tm, aw), lambda j, i: (j, 0)),
        pl.BlockSpec((tm, aw), lambda j, i: (j, 0)),
        pl.BlockSpec((aw, tm), lambda j, i: (0, j)),
        pl.BlockSpec((aw, tm), lambda j, i: (0, j)),
    ]
    args = [x, g, w_in, w_kt, w_in, w_in, cos, sin, cost, sint]
    row_spec = pl.BlockSpec((None, tm, aw), lambda j, i: (i, j, 0))
    col_spec = pl.BlockSpec((None, aw, tm), lambda j, i: (i, 0, j))
    out_specs = [row_spec, col_spec, col_spec,
                 pl.BlockSpec((None, tm * N_HEADS, LANES), lambda j, i: (i, j, 0)),
                 row_spec, row_spec]
    out_shape = [jax.ShapeDtypeStruct((b, l, aw), BF16),
                 jax.ShapeDtypeStruct((b, aw, l), F32),
                 jax.ShapeDtypeStruct((b, aw, l), BF16),
                 jax.ShapeDtypeStruct((b, l * N_HEADS, LANES), F32),
                 jax.ShapeDtypeStruct((b, l, aw), BF16),
                 jax.ShapeDtypeStruct((b, l, SSM_WIDTH), F32)]
    if k_rows:
        in_specs.append(wcol(1))
        args.append(w_in)
        out_specs.append(row_spec)
        out_shape.append(jax.ShapeDtypeStruct((b, l, aw), F32))
    aliases = {}
    if stacked is not None:
        for buf, out_idx in zip(stacked, (1, 3)):
            aliases[len(args)] = out_idx
            in_specs.append(pl.BlockSpec(memory_space=pl.ANY))
            args.append(buf)
            out_shape[out_idx] = jax.ShapeDtypeStruct(buf.shape, buf.dtype)
        out_specs[1] = pl.BlockSpec((None, None, aw, tm), lambda j, i: (layer, i, 0, j))
        out_specs[3] = pl.BlockSpec((None, None, tm * N_HEADS, LANES), lambda j, i: (layer, i, j, 0))
    return pl.pallas_call(
        functools.partial(_inproj_body, k_rows=k_rows, n_aliased=len(aliases)),
        grid=(l // tm, b),
        in_specs=in_specs,
        out_specs=out_specs,
        out_shape=out_shape,
        input_output_aliases=aliases,
        compiler_params=_params("parallel", "parallel"),
        name="in_proj",
    )(*args)


def _lambda(lam_ref, lam_init):
    a = jnp.sum(lam_ref[0:1, :] * lam_ref[1:2, :], axis=-1, keepdims=True)
    b = jnp.sum(lam_ref[2:3, :] * lam_ref[3:4, :], axis=-1, keepdims=True)
    return jnp.exp(a) - jnp.exp(b) + lam_init


def _head_norm(o, subln, lam_init):
    return _rms(o, subln) * (1.0 - lam_init)


def _attn_body(lam_ref, subln_ref, q_ref, kt_ref, v_ref, o_ref, s_scr, *, lam_init, tq, tk):
    n_tiles = q_ref.shape[0] // tq
    lam = _lambda(lam_ref, lam_init)
    subln = subln_ref[...]
    lane = lax.broadcasted_iota(jnp.int32, (tq, LANES), 1)
    row = lax.broadcasted_iota(jnp.int32, (tq, tk), 0)
    col = lax.broadcasted_iota(jnp.int32, (tq, tk), 1)

    def one_map(i, which, qm):
        n_kv = -(-(i + 1) * tq // tk)
        m_tile = jnp.full((tq, LANES), -jnp.inf, F32)
        for j in range(n_kv):
            s = _dot(qm, kt_ref[:, j * tk:(j + 1) * tk])
            if j == n_kv - 1:
                s = jnp.where(col + j * tk <= row + i * tq, s, -jnp.inf)
            s_scr[which, :, j * tk:(j + 1) * tk] = s
            for c in range(tk // LANES):
                m_tile = jnp.maximum(m_tile, s[:, c * LANES:(c + 1) * LANES])
        m_b = jnp.broadcast_to(jnp.max(m_tile, axis=-1, keepdims=True), (tq, LANES))
        l_tile = jnp.zeros((tq, LANES), F32)
        acc = jnp.zeros((tq, LANES), F32)
        for j in range(n_kv):
            parts = []
            for c in range(tk // LANES):
                p = jnp.exp(s_scr[which, :, j * tk + c * LANES:j * tk + (c + 1) * LANES] - m_b)
                l_tile = l_tile + p
                parts.append(p.astype(BF16))
            acc = acc + _dot(jnp.concatenate(parts, axis=1), v_ref[j * tk:(j + 1) * tk, :])
        return acc / jnp.sum(l_tile, axis=-1, keepdims=True)

    for i in range(n_tiles):
        rows = slice(i * tq, (i + 1) * tq)
        q = q_ref[rows, :]
        zero = jnp.zeros_like(q)
        o1 = one_map(i, 0, jnp.where(lane < HEAD_DIM, q, zero))
        o2 = one_map(i, 1, jnp.where(lane >= HEAD_DIM, q, zero))
        o_ref[rows, :] = _head_norm(o1 - lam * o2, subln, lam_init).astype(o_ref.dtype)


def _attn_prompt(q, ktb, vb, layer, lam_rows, subln, lam_init, tq=256, tk=256):
    b, l, _ = q.shape
    assert tk % tq == 0 and l % tk == 0
    rows = pl.BlockSpec((None, l, LANES), lambda i, h: (i, 0, h))
    return pl.pallas_call(
        functools.partial(_attn_body, lam_init=lam_init, tq=tq, tk=tk),
        grid=(b, N_HEADS),
        in_specs=[
            _const_spec((None, 4, HEAD_DIM), (layer, 0, 0)),
            _const_spec((None, 1, LANES), (layer, 0, 0)),
            rows,
            pl.BlockSpec((None, LANES, l), lambda i, h: (i, h, 0)),
            rows,
        ],
        out_specs=rows,
        out_shape=jax.ShapeDtypeStruct((b, l, ATTN_WIDTH), BF16),
        scratch_shapes=[pltpu.VMEM((2, tq, l), F32)],
        compiler_params=_params("parallel", "parallel"),
        name="attn_prompt",
    )(lam_rows, subln, q, ktb, vb)


def _page_copies(pt_ref, ckt_hbm, cv_hbm, kbuf, vbuf, sems, layer, sample, slot):
    copies = []
    for p in range(kbuf.shape[1]):
        page = pt_ref[sample, p]
        copies.append(pltpu.make_async_copy(ckt_hbm.at[layer, page], kbuf.at[slot, p], sems.at[0, slot, p]))
        copies.append(pltpu.make_async_copy(cv_hbm.at[layer, page], vbuf.at[slot, p], sems.at[1, slot, p]))
    return copies


def _decode_body(pt_ref, lam_ref, subln_ref, q_ref, knew_ref, vnew_ref, ckt_hbm, cv_hbm, o_ref,
                 kbuf, vbuf, sems, *, lam_init, layer):
    n_pages = kbuf.shape[1]
    i = pl.program_id(0)
    slot = lax.rem(i, 2)
    copies = functools.partial(_page_copies, pt_ref, ckt_hbm, cv_hbm, kbuf, vbuf, sems, layer)

    @pl.when(i == 0)
    def _():
        for c in copies(0, 0):
            c.start()

    @pl.when(i + 1 < pl.num_programs(0))
    def _():
        for c in copies(i + 1, 1 - slot):
            c.start()

    for c in copies(i, slot):
        c.wait()
    k_refs = [kbuf.at[slot, p] for p in range(n_pages)]
    v_refs = [vbuf.at[slot, p] for p in range(n_pages)]
    n_sub = 2 * N_HEADS
    row = lax.broadcasted_iota(jnp.int32, (n_sub, ATTN_WIDTH), 0)
    lane = lax.broadcasted_iota(jnp.int32, (n_sub, ATTN_WIDTH), 1)
    sub = jnp.where(row < N_HEADS, 2 * row, 2 * (row - N_HEADS) + 1)
    q_blk = jnp.where(lane // HEAD_DIM == sub, q_ref[...].astype(F32), 0.0)
    q_bf = q_blk.astype(BF16)
    s_new = jnp.sum(q_blk * knew_ref[...], axis=-1, keepdims=True)
    scores = [_dot(q_bf, k_refs[p][...].astype(BF16)) for p in range(n_pages)]
    m = jnp.maximum(s_new, jnp.max(_tree(jnp.maximum, scores), axis=-1, keepdims=True))
    p_new = jnp.exp(s_new - m)
    probs = [jnp.exp(s - m) for s in scores]
    l = p_new + jnp.sum(_tree(jnp.add, probs), axis=-1, keepdims=True)
    lam = _lambda(lam_ref, lam_init)
    inv = 1.0 / l
    r8 = lax.broadcasted_iota(jnp.int32, (n_sub, 1), 0)
    coef = jnp.where(r8 < N_HEADS, inv, -lam * inv)

    def fold(x):
        y = x * coef
        y = y + pltpu.roll(y, N_HEADS, 0)
        return jnp.where(lax.broadcasted_iota(jnp.int32, y.shape, 0) < N_HEADS, y, 0.0)

    w_new = fold(jnp.broadcast_to(p_new, (n_sub, LANES)))
    vnew = vnew_ref[...]
    vnew8 = jnp.concatenate([vnew, jnp.zeros_like(vnew)], axis=0)
    terms = [w_new * vnew8]
    rsel = lax.broadcasted_iota(jnp.int32, (n_sub, PAGE_SIZE), 0)
    for p in range(n_pages):
        w = fold(probs[p])
        for h in range(N_HEADS):
            wh = jnp.where(rsel == h, w, 0.0).astype(BF16)
            vh = v_refs[p][pl.ds(h, PAGE_SIZE, stride=N_HEADS), :].astype(BF16)
            terms.append(_dot(wh, vh))
    acc = _tree(jnp.add, terms)
    o = _head_norm(acc[:N_HEADS], subln_ref[...], lam_init)
    o_ref[...] = o.astype(o_ref.dtype)


def _attn_decode(q, k_new, v_new4, cache_kt, cache_v, page_table, layer, lam_rows, subln, lam_init):
    b = q.shape[0]
    n_pages = page_table.shape[1]
    page_shape = (2, n_pages, ATTN_WIDTH, LANES)
    in_specs = [
        pl.BlockSpec((None, 4, HEAD_DIM), lambda i, pt: (layer, 0, 0)),
        pl.BlockSpec((None, 1, LANES), lambda i, pt: (layer, 0, 0)),
        pl.BlockSpec((None, 1, ATTN_WIDTH), lambda i, pt: (i, 0, 0)),
        pl.BlockSpec((None, 1, ATTN_WIDTH), lambda i, pt: (i, 0, 0)),
        pl.BlockSpec((None, N_HEADS, LANES), lambda i, pt: (i, 0, 0)),
        pl.BlockSpec(memory_space=pl.ANY),
        pl.BlockSpec(memory_space=pl.ANY),
    ]
    grid_spec = pltpu.PrefetchScalarGridSpec(
        num_scalar_prefetch=1,
        grid=(b,),
        in_specs=in_specs,
        out_specs=pl.BlockSpec((None, N_HEADS, LANES), lambda i, pt: (i, 0, 0)),
        scratch_shapes=[pltpu.VMEM(page_shape, F32), pltpu.VMEM(page_shape, F32),
                        pltpu.SemaphoreType.DMA((2, 2, n_pages))],
    )
    return pl.pallas_call(
        functools.partial(_decode_body, lam_init=lam_init, layer=layer),
        grid_spec=grid_spec,
        out_shape=jax.ShapeDtypeStruct((b, N_HEADS, LANES), BF16),
        compiler_params=_params("arbitrary"),
        name="attn_decode",
    )(page_table, lam_rows, subln, q, k_new, v_new4, cache_kt, cache_v)


def _cmul(ar, ai, br, bi):
    return ar * br - ai * bi, ar * bi + ai * br


def _ssm_prep_body(are_ref, aim_ref, ldt_ref, btr_ref, bti_ref, ctr_ref, cti_ref,
                   wk_ref, ws_ref, wc_ref, wc0_ref, dec_ref):
    t = T_CHUNK
    a_re = are_ref[...]
    a_im = aim_ref[...]
    dt = jnp.exp(ldt_ref[...])
    mag = jnp.exp(a_re * dt)
    p_re = mag * jnp.cos(a_im * dt)
    p_im = mag * jnp.sin(a_im * dt)
    den = a_re * a_re + a_im * a_im
    nr = p_re - 1.0
    coef_re = (nr * a_re + p_im * a_im) / den
    coef_im = (p_im * a_re - nr * a_im) / den

    r = lax.broadcasted_iota(jnp.int32, (LANES, SG_STATE), 0)
    c = lax.broadcasted_iota(jnp.int32, (LANES, SG_STATE), 1)
    bt_mask = r // GROUP_SIZE == c // STATE_DIM
    r = lax.broadcasted_iota(jnp.int32, (SG_STATE, LANES), 0)
    c = lax.broadcasted_iota(jnp.int32, (SG_STATE, LANES), 1)
    ct_mask = r // STATE_DIM == c // GROUP_SIZE
    btr = jnp.where(bt_mask, btr_ref[...], 0.0)
    bti = jnp.where(bt_mask, bti_ref[...], 0.0)
    ctr = jnp.where(ct_mask, ctr_ref[...], 0.0)
    cti = jnp.where(ct_mask, cti_ref[...], 0.0)
    bbr, bbi = _cmul(coef_re, coef_im, btr, bti)

    pow_re = [jnp.ones_like(p_re)]
    pow_im = [jnp.zeros_like(p_im)]
    col = lambda x: jnp.transpose(jnp.broadcast_to(x, (LANES, SG_STATE)))
    p_re_c, p_im_c = col(p_re), col(p_im)
    cpow_re = [jnp.ones_like(p_re_c)]
    cpow_im = [jnp.zeros_like(p_im_c)]
    for _ in range(t):
        nr_, ni_ = _cmul(pow_re[-1], pow_im[-1], p_re, p_im)
        pow_re.append(nr_)
        pow_im.append(ni_)
        nr_, ni_ = _cmul(cpow_re[-1], cpow_im[-1], p_re_c, p_im_c)
        cpow_re.append(nr_)
        cpow_im.append(ni_)

    ct_cat = jnp.concatenate([ctr, cti], axis=0)
    wk_ref[...] = jnp.zeros(wk_ref.shape, wk_ref.dtype)
    for j in range(t):
        er, ei = _cmul(bbr, bbi, pow_re[j], pow_im[j])
        kj = jnp.dot(jnp.concatenate([er, -ei], axis=1), ct_cat, precision=_HI,
                     preferred_element_type=F32).astype(BF16)
        for t_in in range(t - j):
            wk_ref[t_in * LANES:(t_in + 1) * LANES, (t_in + j) * LANES:(t_in + j + 1) * LANES] = kj
        rows = slice((t - 1 - j) * LANES, (t - j) * LANES)
        ws_ref[rows, :SG_STATE] = er.astype(BF16)
        ws_ref[rows, SG_STATE:] = ei.astype(BF16)
    for j in range(1, t + 1):
        cols = slice((j - 1) * LANES, j * LANES)
        wc_ref[:SG_STATE, cols] = (ctr * cpow_re[j] - cti * cpow_im[j]).astype(BF16)
        wc_ref[SG_STATE:, cols] = (-(ctr * cpow_im[j] + cti * cpow_re[j])).astype(BF16)
    wc0_ref[:SG_STATE, :] = ctr.astype(BF16)
    wc0_ref[SG_STATE:, :] = (-cti).astype(BF16)

    dec_rows = [jnp.concatenate([p_re, p_im], axis=1)]
    dr, di = pow_re[t], pow_im[t]
    for _ in range(N_DEC_ROWS - 1):
        dec_rows.append(jnp.concatenate([dr, di], axis=1))
        dr, di = _cmul(dr, di, dr, di)
    dec_ref[...] = jnp.concatenate(dec_rows, axis=0)


def _ssm_prep(a_re, a_im, log_dt, b_re, b_im, c_re, c_im):
    depth = a_re.shape[0]
    rowed = lambda x: x.reshape(depth, N_SG, 1, SG_STATE)
    ldt = jnp.broadcast_to(log_dt[:, :, None], (depth, N_GROUPS, STATE_DIM))

    def bt(x):
        y = x.reshape(depth, N_SG, SG_GROUPS, STATE_DIM, GROUP_SIZE).transpose(0, 1, 4, 2, 3)
        y = y.reshape(depth, N_SG, 1, GROUP_SIZE, SG_STATE)
        return jnp.broadcast_to(y, (depth, N_SG, SG_GROUPS, GROUP_SIZE, SG_STATE)).reshape(
            depth, N_SG, LANES, SG_STATE)

    def ct(x):
        y = x.reshape(depth, N_SG, SG_GROUPS, GROUP_SIZE, STATE_DIM).transpose(0, 1, 4, 2, 3)
        y = y.reshape(depth, N_SG, 1, STATE_DIM, LANES)
        return jnp.broadcast_to(y, (depth, N_SG, SG_GROUPS, STATE_DIM, LANES)).reshape(
            depth, N_SG, SG_STATE, LANES)

    tk = T_CHUNK * LANES
    blk = lambda *shape: pl.BlockSpec((None, None) + shape, lambda l, g: (l, g, 0, 0))
    return pl.pallas_call(
        _ssm_prep_body,
        grid=(depth, N_SG),
        in_specs=[blk(1, SG_STATE)] * 3 + [blk(LANES, SG_STATE)] * 2 + [blk(SG_STATE, LANES)] * 2,
        out_specs=[blk(tk, tk), blk(tk, 2 * SG_STATE), blk(2 * SG_STATE, tk),
                   blk(2 * SG_STATE, LANES), blk(N_DEC_ROWS, 2 * SG_STATE)],
        out_shape=[jax.ShapeDtypeStruct((depth, N_SG, tk, tk), BF16),
                   jax.ShapeDtypeStruct((depth, N_SG, tk, 2 * SG_STATE), BF16),
                   jax.ShapeDtypeStruct((depth, N_SG, 2 * SG_STATE, tk), BF16),
                   jax.ShapeDtypeStruct((depth, N_SG, 2 * SG_STATE, LANES), BF16),
                   jax.ShapeDtypeStruct((depth, N_SG, N_DEC_ROWS, 2 * SG_STATE), F32)],
        compiler_params=_params("parallel", "parallel"),
        name="ssm_prep",
    )(rowed(a_re), rowed(a_im), rowed(ldt), bt(b_re), bt(b_im), ct(c_re), ct(c_im))


def _ssm_prompt_body(us_ref, wk_ref, ws_ref, wc_ref, dec_ref, y_ref, st_ref):
    t = T_CHUNK
    n_chunks = us_ref.shape[0] // t
    x = jnp.concatenate([us_ref[pl.ds(i, n_chunks, stride=t), :] for i in range(t)], axis=1).astype(BF16)
    h_loc = _dot(x, ws_ref[...])
    s_re = h_loc[:, :SG_STATE]
    s_im = h_loc[:, SG_STATE:]
    row = lax.broadcasted_iota(jnp.int32, s_re.shape, 0)
    shifted = lambda v, d: jnp.where(row >= d, pltpu.roll(v, d, 0), 0.0)
    d = 1
    i = 1
    while d < n_chunks:
        a_r = dec_ref[i:i + 1, :SG_STATE]
        a_i = dec_ref[i:i + 1, SG_STATE:]
        add_re, add_im = _cmul(a_r, a_i, shifted(s_re, d), shifted(s_im, d))
        s_re = s_re + add_re
        s_im = s_im + add_im
        d *= 2
        i += 1
    st_ref[0:1, :] = s_re[n_chunks - 1:n_chunks, :]
    st_ref[1:2, :] = s_im[n_chunks - 1:n_chunks, :]
    h_in = jnp.concatenate([shifted(s_re, 1), shifted(s_im, 1)], axis=1).astype(BF16)
    y = _dot(x, wk_ref[...]) + _dot(h_in, wc_ref[...])
    for i in range(t):
        y_ref[pl.ds(i, n_chunks, stride=t), :] = y[:, i * LANES:(i + 1) * LANES]


def _ssm_prompt(us, layer, mats):
    wk, ws, wc, _, dec = mats
    b, l, _ = us.shape
    assert l // T_CHUNK <= 2 ** (N_DEC_ROWS - 1)
    tk = T_CHUNK * LANES
    mat = lambda r, c: pl.BlockSpec((None, None, r, c), lambda g, i: (layer, g, 0, 0))
    return pl.pallas_call(
        _ssm_prompt_body,
        grid=(N_SG, b),
        in_specs=[pl.BlockSpec((None, l, LANES), lambda g, i: (i, 0, g)),
                  mat(tk, tk), mat(tk, 2 * SG_STATE), mat(2 * SG_STATE, tk),
                  mat(N_DEC_ROWS, 2 * SG_STATE)],
        out_specs=[pl.BlockSpec((None, l, LANES), lambda g, i: (i, 0, g)),
                   pl.BlockSpec((None, None, 2, SG_STATE), lambda g, i: (i, g, 0, 0))],
        out_shape=[jax.ShapeDtypeStruct((b, l, SSM_WIDTH), F32),
                   jax.ShapeDtypeStruct((b, N_SG, 2, SG_STATE), F32)],
        compiler_params=_params("parallel", "parallel"),
        name="ssm_prompt",
    )(us, wk, ws, wc, dec)


def _ssm_sample_body(us_ref, h0r_ref, h0i_ref, ws_ref, wc0_ref, dec_ref, y_ref, hr_ref, hi_ref):
    bu = _dot(us_ref[...].astype(BF16), ws_ref[...])
    a_r = dec_ref[0:1, :SG_STATE]
    a_i = dec_ref[0:1, SG_STATE:]
    dr, di = _cmul(a_r, a_i, h0r_ref[...], h0i_ref[...])
    h_re = bu[:, :SG_STATE] + dr
    h_im = bu[:, SG_STATE:] + di
    hr_ref[...] = h_re
    hi_ref[...] = h_im
    y_ref[...] = _dot(jnp.concatenate([h_re, h_im], axis=1).astype(BF16), wc0_ref[...])


def _ssm_sample(us, h0_re, h0_im, layer, mats):
    _, ws, _, wc0, dec = mats
    b = us.shape[0]
    st_spec = pl.BlockSpec((None, b, SG_STATE), lambda g: (g, 0, 0))
    return pl.pallas_call(
        _ssm_sample_body,
        grid=(N_SG,),
        in_specs=[pl.BlockSpec((b, LANES), lambda g: (0, g)), st_spec, st_spec,
                  pl.BlockSpec((None, None, LANES, 2 * SG_STATE), lambda g: (layer, g, T_CHUNK - 1, 0)),
                  pl.BlockSpec((None, None, 2 * SG_STATE, LANES), lambda g: (layer, g, 0, 0)),
                  pl.BlockSpec((None, None, N_DEC_ROWS, 2 * SG_STATE), lambda g: (layer, g, 0, 0))],
        out_specs=[pl.BlockSpec((b, LANES), lambda g: (0, g)), st_spec, st_spec],
        out_shape=[jax.ShapeDtypeStruct((b, SSM_WIDTH), F32),
                   jax.ShapeDtypeStruct((N_SG, b, SG_STATE), F32),
                   jax.ShapeDtypeStruct((N_SG, b, SG_STATE), F32)],
        compiler_params=_params("parallel"),
        name="ssm_sample",
    )(us, h0_re, h0_im, ws, wc0, dec)


def _gelu_tanh(x):
    return 0.5 * x * (1.0 + jnp.tanh(math.sqrt(2.0 / math.pi) * (x + 0.044715 * (x * x * x))))


def _merge_body(h_ref, o_ref, y_ref, us_ref, g_ref, wgate_ref, bgate_ref, d_ref, gluw_ref, glub_ref,
                wa_ref, ws_ref, wo_ref, out_ref):
    h = h_ref[...]
    u = _rms(h, g_ref[...]).astype(BF16)
    gates = jax.nn.sigmoid(_dot(u, wgate_ref[...]) + bgate_ref[...])
    y = _gelu_tanh(y_ref[...] + d_ref[...] * us_ref[...])
    ys = y * jax.nn.sigmoid(_dot(y.astype(BF16), gluw_ref[...]) + glub_ref[...])
    a = _dot(o_ref[...], wa_ref[...])
    s = _dot(ys.astype(BF16), ws_ref[...])
    mix = gates[:, :D_MODEL] * a + gates[:, D_MODEL:] * s
    out_ref[...] = h + _dot(mix.astype(BF16), wo_ref[...])


def _merge(h, o, y, us, layer, w, tm=512):
    m = h.shape[0]
    tm = min(tm, m)
    row = lambda n: pl.BlockSpec((tm, n), lambda i: (i, 0))
    return pl.pallas_call(
        _merge_body,
        grid=(m // tm,),
        in_specs=[row(D_MODEL), row(ATTN_WIDTH), row(SSM_WIDTH), row(SSM_WIDTH),
                  _const_spec((None, 1, D_MODEL), (layer, 0, 0)),
                  _const_spec((None, D_MODEL, 2 * D_MODEL), (layer, 0, 1)),
                  _const_spec((None, 1, 2 * D_MODEL), (layer, 0, 0)),
                  _const_spec((None, 1, SSM_WIDTH), (layer, 0, 0)),
                  _const_spec((None, SSM_WIDTH, SSM_WIDTH), (layer, 0, 0)),
                  _const_spec((None, 1, SSM_WIDTH), (layer, 0, 0)),
                  _const_spec((None, ATTN_WIDTH, D_MODEL), (layer, 0, 0)),
                  _const_spec((None, SSM_WIDTH, D_MODEL), (layer, 0, 0)),
                  _const_spec((None, D_MODEL, D_MODEL), (layer, 0, 0))],
        out_specs=row(D_MODEL),
        out_shape=jax.ShapeDtypeStruct((m, D_MODEL), F32),
        compiler_params=_params("parallel"),
        name="merge",
    )(h, o, y, us, w["norm_mix"], w["w_in"], w["b_gate"], w["ssm_d"], w["glu_w"], w["glu_b"],
      w["w_attn_out"], w["w_ssm_out"], w["w_out"])


def _rope_tables(pos):
    half = HEAD_DIM // 2
    inv = ROPE_THETA ** (-jnp.arange(half, dtype=F32) / half)
    ang = pos.astype(F32)[:, None] * inv[None, :]
    cos = jnp.cos(ang)
    sin = jnp.sin(ang)
    reps = ATTN_WIDTH // HEAD_DIM
    cos = jnp.tile(jnp.concatenate([cos, cos], axis=1), (1, reps))
    sin = jnp.tile(jnp.concatenate([-sin, sin], axis=1), (1, reps))
    return cos, sin, cos.T, sin.T


def kernel(x_prompt, x_sample, cache_k, cache_v, state_ssm_re, state_ssm_im, page_table, norm_ffn1, ffn1_w_gate, ffn1_w_up, ffn1_w_down, norm_mix, w_in, b_gate, lambda_q1, lambda_k1, lambda_q2, lambda_k2, attn_subln, w_attn_out, ssm_a_re, ssm_a_im, ssm_log_dt, ssm_b_re, ssm_b_im, ssm_c_re, ssm_c_im, ssm_d, glu_w, glu_b, w_ssm_out, w_out, norm_ffn2, ffn2_w_gate, ffn2_w_up, ffn2_w_down, final_norm):
    batch, seq, _ = x_prompt.shape
    dec_batch, dec_seq, _ = x_sample.shape
    depth = w_in.shape[0]
    n_pool = cache_k.shape[1]
    past_len = page_table.shape[1] * PAGE_SIZE
    assert dec_seq == 1

    vec = lambda x: x.reshape(depth, 1, -1)
    bf = lambda x: x.astype(BF16)
    w_in_b = bf(w_in)
    w_kt = bf(jnp.swapaxes(w_in[:, :, ATTN_WIDTH:2 * ATTN_WIDTH], 1, 2))
    mw = dict(norm_mix=vec(norm_mix), w_in=w_in_b, b_gate=vec(b_gate), ssm_d=vec(ssm_d), glu_w=bf(glu_w),
              glu_b=vec(glu_b), w_attn_out=bf(w_attn_out), w_ssm_out=bf(w_ssm_out), w_out=bf(w_out))
    f1 = (vec(norm_ffn1), bf(ffn1_w_gate), bf(ffn1_w_up), bf(ffn1_w_down))
    f2 = (vec(norm_ffn2), bf(ffn2_w_gate), bf(ffn2_w_up), bf(ffn2_w_down))
    lam_rows = jnp.stack([lambda_q1, lambda_k1, lambda_q2, lambda_k2], axis=1)
    subln = vec(attn_subln)
    final_g = final_norm.reshape(1, D_MODEL)

    tabs_p = _rope_tables(jnp.arange(seq, dtype=jnp.int32))
    tabs_s = _rope_tables(jnp.full((dec_batch,), past_len, jnp.int32))
    cache_kt = jnp.transpose(cache_k, (0, 1, 3, 4, 2)).reshape(depth, n_pool, ATTN_WIDTH, PAGE_SIZE)
    cache_v2 = cache_v.reshape(depth, n_pool, PAGE_SIZE * N_HEADS, 2 * HEAD_DIM)

    mats = _ssm_prep(ssm_a_re, ssm_a_im, ssm_log_dt, ssm_b_re, ssm_b_im, ssm_c_re, ssm_c_im)
    sg_state = lambda x: x.reshape(dec_batch, N_SG, SG_STATE).transpose(1, 0, 2)
    sg_unstate = lambda x: x.transpose(1, 0, 2).reshape(dec_batch, N_GROUPS, STATE_DIM)

    xp = x_prompt.reshape(batch * seq, D_MODEL)
    xs = x_sample.reshape(dec_batch, D_MODEL)
    outs = [[] for _ in range(8)]
    kt_all = jnp.zeros((depth, batch, ATTN_WIDTH, seq), F32)
    v4_all = jnp.zeros((depth, batch, seq * N_HEADS, 2 * HEAD_DIM), F32)
    for l in range(depth):
        lam_init = 0.8 - 0.6 * math.exp(-0.3 * l)
        last = l == depth - 1

        h = _ffn(xp, l, *f1)
        q, kt_all, ktb, v4_all, vb, us = _inproj(h.reshape(batch, seq, D_MODEL), l, mw["norm_mix"], w_in_b, w_kt,
                                                 tabs_p, stacked=(kt_all, v4_all))
        o = _attn_prompt(q, ktb, vb, l, lam_rows, subln, lam_init)
        y, st = _ssm_prompt(us, l, mats)
        h = _merge(h, o.reshape(batch * seq, ATTN_WIDTH), y.reshape(batch * seq, SSM_WIDTH),
                   us.reshape(batch * seq, SSM_WIDTH), l, mw)
        xp = _ffn(h, l, *f2, final_g=final_g if last else None)
        outs[2].append(st[:, :, 0, :].reshape(batch, N_GROUPS, STATE_DIM))
        outs[3].append(st[:, :, 1, :].reshape(batch, N_GROUPS, STATE_DIM))

        h = _ffn(xs, l, *f1)
        q, kt, _, v4, _, us, k_new = _inproj(h.reshape(1, dec_batch, D_MODEL), l, mw["norm_mix"], w_in_b, w_kt,
                                             tabs_s, k_rows=True)
        v_new4 = v4.reshape(dec_batch, N_HEADS, 2 * HEAD_DIM)
        o = _attn_decode(q.reshape(dec_batch, 1, ATTN_WIDTH), k_new.reshape(dec_batch, 1, ATTN_WIDTH), v_new4,
                         cache_kt, cache_v2, page_table, l, lam_rows, subln, lam_init)
        us = us.reshape(dec_batch, SSM_WIDTH)
        y, h_re, h_im = _ssm_sample(us, sg_state(state_ssm_re[l]), sg_state(state_ssm_im[l]), l, mats)
        h = _merge(h, o.reshape(dec_batch, ATTN_WIDTH), y, us, l, mw)
        xs = _ffn(h, l, *f2, final_g=final_g if last else None)
        outs[4].append(kt.reshape(2 * N_HEADS, HEAD_DIM, dec_batch).transpose(2, 0, 1)[:, None])
        outs[5].append(v_new4[:, None])
        outs[6].append(sg_unstate(h_re))
        outs[7].append(sg_unstate(h_im))

    stk = [jnp.stack(o) for o in outs[2:]]
    k_prompt = kt_all.reshape(depth, batch, 2 * N_HEADS, HEAD_DIM, seq).transpose(0, 1, 4, 2, 3)
    v_prompt = v4_all.reshape(depth, batch, seq, N_HEADS, 2 * HEAD_DIM)
    return (xp.reshape(batch, seq, D_MODEL), xs.reshape(dec_batch, dec_seq, D_MODEL),
            k_prompt, v_prompt, stk[0], stk[1], stk[2], stk[3], stk[4], stk[5])
```

```python
import functools
import math

import jax
import jax.numpy as jnp
from jax import lax
from jax.experimental import pallas as pl
from jax.experimental.pallas import tpu as pltpu

F32 = jnp.float32
BF16 = jnp.bfloat16

D_MODEL = 1024
N_HEADS = 4
HEAD_DIM = 64
ATTN_WIDTH = 2 * N_HEADS * HEAD_DIM
SSM_WIDTH = 512
GROUP_SIZE = 16
N_GROUPS = SSM_WIDTH // GROUP_SIZE
STATE_DIM = 64
D_FF = 2816
ROPE_THETA = 10000.0
NORM_EPS = 1e-6
PAGE_SIZE = 128

LANES = 128
SG_GROUPS = LANES // GROUP_SIZE
N_SG = N_GROUPS // SG_GROUPS
SG_STATE = SG_GROUPS * STATE_DIM
T_CHUNK = 8
N_DEC_ROWS = 16
FF_CHUNK = 256
VMEM_LIMIT = 56 * 2**20

_HI = lax.Precision.HIGHEST


def _dot(a, b):
    return jnp.dot(a, b, preferred_element_type=F32)


def _rms(x, g):
    return x * lax.rsqrt(jnp.mean(x * x, axis=-1, keepdims=True) + NORM_EPS) * g


def _tree(op, xs):
    xs = list(xs)
    while len(xs) > 1:
        xs = [op(xs[k], xs[k + 1]) if k + 1 < len(xs) else xs[k] for k in range(0, len(xs), 2)]
    return xs[0]


def _params(*sem):
    return pltpu.CompilerParams(dimension_semantics=sem, vmem_limit_bytes=VMEM_LIMIT)


def _const_spec(shape, index):
    return pl.BlockSpec(shape, lambda *_: index, pipeline_mode=pl.Buffered(1))


def _swiglu_chunks(u, wg_ref, wu_ref, wd_ref, acc, first, last):
    for c in range(first, last):
        cols = slice(c * FF_CHUNK, (c + 1) * FF_CHUNK)
        gate = _dot(u, wg_ref[:, cols])
        up = _dot(u, wu_ref[:, cols])
        act = (gate * jax.nn.sigmoid(gate) * up).astype(BF16)
        acc = acc + _dot(act, wd_ref[cols, :])
    return acc


def _ffn_body(*refs, final):
    if final:
        x_ref, g_ref, wg_ref, wu_ref, wd_ref, fg_ref, o_ref = refs
    else:
        x_ref, g_ref, wg_ref, wu_ref, wd_ref, o_ref = refs
    x = x_ref[...]
    u = _rms(x, g_ref[...]).astype(BF16)
    acc = _swiglu_chunks(u, wg_ref, wu_ref, wd_ref, jnp.zeros(x.shape, F32), 0, D_FF // FF_CHUNK)
    y = x + 0.5 * acc
    if final:
        y = _rms(y, fg_ref[...])
    o_ref[...] = y


def _ffn(x, layer, g, wg, wu, wd, final_g=None, tm=512):
    m = x.shape[0]
    tm = min(tm, m)
    in_specs = [
        pl.BlockSpec((tm, D_MODEL), lambda i: (i, 0)),
        _const_spec((None, 1, D_MODEL), (layer, 0, 0)),
        _const_spec((None, D_MODEL, D_FF), (layer, 0, 0)),
        _const_spec((None, D_MODEL, D_FF), (layer, 0, 0)),
        _const_spec((None, D_FF, D_MODEL), (layer, 0, 0)),
    ]
    args = [x, g, wg, wu, wd]
    if final_g is not None:
        in_specs.append(_const_spec((1, D_MODEL), (0, 0)))
        args.append(final_g)
    return pl.pallas_call(
        functools.partial(_ffn_body, final=final_g is not None),
        grid=(m // tm,),
        in_specs=in_specs,
        out_specs=pl.BlockSpec((tm, D_MODEL), lambda i: (i, 0)),
        out_shape=jax.ShapeDtypeStruct((m, D_MODEL), F32),
        compiler_params=_params("parallel"),
        name="ffn",
    )(*args)


def _swap_halves_lanes(x):
    n = x.shape[-1]
    half = HEAD_DIM // 2
    lane = lax.broadcasted_iota(jnp.int32, x.shape, x.ndim - 1)
    return jnp.where(lane % HEAD_DIM < half, pltpu.roll(x, n - half, x.ndim - 1),
                     pltpu.roll(x, half, x.ndim - 1))


def _swap_halves_rows(x):
    half = HEAD_DIM // 2
    parts = []
    for r in range(0, x.shape[0], HEAD_DIM):
        parts += [x[r + half:r + HEAD_DIM], x[r:r + half]]
    return jnp.concatenate(parts, axis=0)


def _inproj_body(*refs, k_rows, n_aliased):
    n_in = 10 + (1 if k_rows else 0) + n_aliased
    x_ref, g_ref, wq_ref, wkt_ref, wv_ref, wus_ref, cos_ref, sin_ref, cost_ref, sint_ref = refs[:10]
    q_ref, kt_ref, ktb_ref, v4_ref, vb_ref, us_ref = refs[n_in:n_in + 6]
    tm = x_ref.shape[0]
    u = _rms(x_ref[...], g_ref[...]).astype(BF16)
    cos = cos_ref[...]
    sin = sin_ref[...]
    q = _dot(u, wq_ref[...])
    q = q * cos + _swap_halves_lanes(q) * sin
    q_ref[...] = (q * (HEAD_DIM ** -0.5)).astype(BF16)
    kt = lax.dot_general(wkt_ref[...], u, (((1,), (1,)), ((), ())), preferred_element_type=F32)
    kt = kt * cost_ref[...] + _swap_halves_rows(kt) * sint_ref[...]
    kt_ref[...] = kt
    ktb_ref[...] = kt.astype(BF16)
    v = _dot(u, wv_ref[...])
    vb_ref[...] = v.astype(BF16)
    for h in range(N_HEADS):
        v4_ref[pl.ds(h, tm, stride=N_HEADS), :] = v[:, h * LANES:(h + 1) * LANES]
    us_ref[...] = _dot(u, wus_ref[...])
    if k_rows:
        wk_ref, k_ref = refs[10], refs[n_in + 6]
        k = _dot(u, wk_ref[...])
        k_ref[...] = k * cos + _swap_halves_lanes(k) * sin


def _inproj(x, layer, g, w_in, w_kt, tabs, k_rows=False, stacked=None, tm=512):
    b, l, _ = x.shape
    tm = min(tm, l)
    cos, sin, cost, sint = tabs
    aw = ATTN_WIDTH
    wcol = lambda c: _const_spec((None, D_MODEL, aw), (layer, 0, c))
    in_specs = [
        pl.BlockSpec((None, tm, D_MODEL), lambda j, i: (i, j, 0)),
        _const_spec((None, 1, D_MODEL), (layer, 0, 0)),
        wcol(0),
        _const_spec((None, aw, D_MODEL), (layer, 0, 0)),
        wcol(2),
        wcol(3),
        pl.BlockSpec((tm, aw), lambda j, i: (j, 0)),
        pl.BlockSpec((tm, aw), lambda j, i: (j, 0)),
        pl.BlockSpec((aw, tm), lambda j, i: (0, j)),
        pl.BlockSpec((aw, tm), lambda j, i: (0, j)),
    ]
    args = [x, g, w_in, w_kt, w_in, w_in, cos, sin, cost, sint]
    row_spec = pl.BlockSpec((None, tm, aw), lambda j, i: (i, j, 0))
    col_spec = pl.BlockSpec((None, aw, tm), lambda j, i: (i, 0, j))
    out_specs = [row_spec, col_spec, col_spec,
                 pl.BlockSpec((None, tm * N_HEADS, LANES), lambda j, i: (i, j, 0)),
                 row_spec, row_spec]
    out_shape = [jax.ShapeDtypeStruct((b, l, aw), BF16),
                 jax.ShapeDtypeStruct((b, aw, l), F32),
                 jax.ShapeDtypeStruct((b, aw, l), BF16),
                 jax.ShapeDtypeStruct((b, l * N_HEADS, LANES), F32),
                 jax.ShapeDtypeStruct((b, l, aw), BF16),
                 jax.ShapeDtypeStruct((b, l, SSM_WIDTH), F32)]
    if k_rows:
        in_specs.append(wcol(1))
        args.append(w_in)
        out_specs.append(row_spec)
        out_shape.append(jax.ShapeDtypeStruct((b, l, aw), F32))
    aliases = {}
    if stacked is not None:
        for buf, out_idx in zip(stacked, (1, 3)):
            aliases[len(args)] = out_idx
            in_specs.append(pl.BlockSpec(memory_space=pl.ANY))
            args.append(buf)
            out_shape[out_idx] = jax.ShapeDtypeStruct(buf.shape, buf.dtype)
        out_specs[1] = pl.BlockSpec((None, None, aw, tm), lambda j, i: (layer, i, 0, j))
        out_specs[3] = pl.BlockSpec((None, None, tm * N_HEADS, LANES), lambda j, i: (layer, i, j, 0))
    return pl.pallas_call(
        functools.partial(_inproj_body, k_rows=k_rows, n_aliased=len(aliases)),
        grid=(l // tm, b),
        in_specs=in_specs,
        out_specs=out_specs,
        out_shape=out_shape,
        input_output_aliases=aliases,
        compiler_params=_params("parallel", "parallel"),
        name="in_proj",
    )(*args)


def _lambda(lam_ref, lam_init):
    a = jnp.sum(lam_ref[0:1, :] * lam_ref[1:2, :], axis=-1, keepdims=True)
    b = jnp.sum(lam_ref[2:3, :] * lam_ref[3:4, :], axis=-1, keepdims=True)
    return jnp.exp(a) - jnp.exp(b) + lam_init


def _head_norm(o, subln, lam_init):
    return _rms(o, subln) * (1.0 - lam_init)


def _attn_body(lam_ref, subln_ref, q_ref, kt_ref, v_ref, o_ref, s_scr, *, lam_init, tq, tk):
    n_tiles = q_ref.shape[0] // tq
    lam = _lambda(lam_ref, lam_init)
    subln = subln_ref[...]
    lane = lax.broadcasted_iota(jnp.int32, (tq, LANES), 1)
    row = lax.broadcasted_iota(jnp.int32, (tq, tk), 0)
    col = lax.broadcasted_iota(jnp.int32, (tq, tk), 1)

    def one_map(i, which, qm):
        n_kv = -(-(i + 1) * tq // tk)
        m_tile = jnp.full((tq, LANES), -jnp.inf, F32)
        for j in range(n_kv):
            s = _dot(qm, kt_ref[:, j * tk:(j + 1) * tk])
            if j == n_kv - 1:
                s = jnp.where(col + j * tk <= row + i * tq, s, -jnp.inf)
            s_scr[which, :, j * tk:(j + 1) * tk] = s
            for c in range(tk // LANES):
                m_tile = jnp.maximum(m_tile, s[:, c * LANES:(c + 1) * LANES])
        m_b = jnp.broadcast_to(jnp.max(m_tile, axis=-1, keepdims=True), (tq, LANES))
        l_tile = jnp.zeros((tq, LANES), F32)
        acc = jnp.zeros((tq, LANES), F32)
        for j in range(n_kv):
            parts = []
            for c in range(tk // LANES):
                p = jnp.exp(s_scr[which, :, j * tk + c * LANES:j * tk + (c + 1) * LANES] - m_b)
                l_tile = l_tile + p
                parts.append(p.astype(BF16))
            acc = acc + _dot(jnp.concatenate(parts, axis=1), v_ref[j * tk:(j + 1) * tk, :])
        return acc / jnp.sum(l_tile, axis=-1, keepdims=True)

    for i in range(n_tiles):
        rows = slice(i * tq, (i + 1) * tq)
        q = q_ref[rows, :]
        zero = jnp.zeros_like(q)
        o1 = one_map(i, 0, jnp.where(lane < HEAD_DIM, q, zero))
        o2 = one_map(i, 1, jnp.where(lane >= HEAD_DIM, q, zero))
        o_ref[rows, :] = _head_norm(o1 - lam * o2, subln, lam_init).astype(o_ref.dtype)


def _attn_prompt(q, ktb, vb, layer, lam_rows, subln, lam_init, tq=256, tk=256):
    b, l, _ = q.shape
    assert tk % tq == 0 and l % tk == 0
    rows = pl.BlockSpec((None, l, LANES), lambda i, h: (i, 0, h))
    return pl.pallas_call(
        functools.partial(_attn_body, lam_init=lam_init, tq=tq, tk=tk),
        grid=(b, N_HEADS),
        in_specs=[
            _const_spec((None, 4, HEAD_DIM), (layer, 0, 0)),
            _const_spec((None, 1, LANES), (layer, 0, 0)),
            rows,
            pl.BlockSpec((None, LANES, l), lambda i, h: (i, h, 0)),
            rows,
        ],
        out_specs=rows,
        out_shape=jax.ShapeDtypeStruct((b, l, ATTN_WIDTH), BF16),
        scratch_shapes=[pltpu.VMEM((2, tq, l), F32)],
        compiler_params=_params("parallel", "parallel"),
        name="attn_prompt",
    )(lam_rows, subln, q, ktb, vb)


def _rows_to_row_sums(tiles):
    row = lax.broadcasted_iota(jnp.int32, tiles[0].shape, 0)
    step = 4
    while len(tiles) > 1:
        half = len(tiles) // 2
        low = (row & step) == 0
        nxt = []
        for j in range(half):
            a, b = tiles[j], tiles[j + half]
            keep = jnp.where(low, a, b)
            other = jnp.where(low, b, a)
            partner = jnp.where(low, pltpu.roll(other, 8 - step, 0), pltpu.roll(other, step, 0))
            nxt.append(keep + partner)
        tiles = nxt
        step //= 2
    return tiles[0]


def _decode_one(q, k_new, v_new, k_refs, v_refs, lam, subln, lam_init):
    n_sub = 2 * N_HEADS
    qf = q.astype(F32)
    row = lax.broadcasted_iota(jnp.int32, (n_sub, ATTN_WIDTH), 0)
    lane = lax.broadcasted_iota(jnp.int32, (n_sub, ATTN_WIDTH), 1)
    q_blk = jnp.where(lane // HEAD_DIM == row, qf, 0.0)
    s_new = jnp.sum(q_blk * k_new, axis=-1, keepdims=True)
    q_col = jnp.transpose(jnp.broadcast_to(qf, (LANES, ATTN_WIDTH)))
    scores = []
    for k_ref in k_refs:
        prod = k_ref[...] * q_col
        tiles = [jnp.sum(prod[s * HEAD_DIM:(s + 1) * HEAD_DIM].reshape(HEAD_DIM // 8, 8, LANES), axis=0)
                 for s in range(n_sub)]
        scores.append(_rows_to_row_sums(tiles))
    m = jnp.maximum(s_new, jnp.max(_tree(jnp.maximum, scores), axis=-1, keepdims=True))
    p_new = jnp.exp(s_new - m)
    probs = [jnp.exp(s - m) for s in scores]
    inv = 1.0 / (p_new + jnp.sum(_tree(jnp.add, probs), axis=-1, keepdims=True))
    r8 = lax.broadcasted_iota(jnp.int32, (n_sub, 1), 0)
    coef = jnp.where(r8 % 2 == 0, inv, -lam * inv)

    def fold(x):
        y = x * coef
        return y + pltpu.roll(y, n_sub - 1, 0)

    weights = [fold(p).astype(BF16) for p in probs]
    assert len(v_refs) % 2 == 0
    pair_sums = []
    for pair in range(N_HEADS // 2):
        terms = []
        for p in range(0, len(v_refs), 2):
            w2 = jnp.concatenate(weights[p:p + 2], axis=1)
            v2 = jnp.concatenate(
                [jnp.concatenate([v_ref[pl.ds(2 * pair + h, PAGE_SIZE, stride=N_HEADS), :].astype(BF16)
                                  for h in range(2)], axis=1) for v_ref in v_refs[p:p + 2]], axis=0)
            terms.append(_dot(w2, v2))
        pair_sums.append(_tree(jnp.add, terms))
    w_new = fold(jnp.broadcast_to(p_new, (n_sub, LANES)))
    heads = []
    for h in range(N_HEADS):
        r, c = 2 * h, (h % 2) * LANES
        heads.append(pair_sums[h // 2][r:r + 1, c:c + LANES] + w_new[r:r + 1, :] * v_new[h:h + 1, :])
    return _head_norm(jnp.concatenate(heads, axis=0), subln, lam_init)


def _page_copies(pt_ref, ckt_hbm, cv_hbm, kbuf, vbuf, sems, layer, sample, slot):
    copies = []
    for p in range(kbuf.shape[1]):
        page = pt_ref[sample, p]
        copies.append(pltpu.make_async_copy(ckt_hbm.at[layer, page], kbuf.at[slot, p], sems.at[0, slot, p]))
        copies.append(pltpu.make_async_copy(cv_hbm.at[layer, page], vbuf.at[slot, p], sems.at[1, slot, p]))
    return copies


DEC_PER_STEP = 2


def _ffn_decode_body(*refs, final, lam_init, layer, sample_base):
    n_in = 14 if final else 13
    pt_ref, x_ref, g_ref, wg_ref, wu_ref, wd_ref = refs[:6]
    fg_ref = refs[6] if final else None
    lam_ref, subln_ref, q_ref, knew_ref, vnew_ref, ckt_hbm, cv_hbm = refs[n_in - 7:n_in]
    y_ref, od_ref, kbuf, vbuf, sems = refs[n_in:]
    n_pages = kbuf.shape[1]
    i = pl.program_id(0)
    more = i + 1 < pl.num_programs(0)
    first = sample_base + DEC_PER_STEP * i
    copies = functools.partial(_page_copies, pt_ref, ckt_hbm, cv_hbm, kbuf, vbuf, sems, layer)
    lam = _lambda(lam_ref, lam_init)
    subln = subln_ref[...]

    def decode(s):
        k_refs = [kbuf.at[s, p] for p in range(n_pages)]
        v_refs = [vbuf.at[s, p] for p in range(n_pages)]
        o = _decode_one(q_ref[s], knew_ref[s], vnew_ref[s], k_refs, v_refs, lam, subln, lam_init)
        od_ref[s] = o.astype(od_ref.dtype)

    @pl.when(i == 0)
    def _():
        for s in range(DEC_PER_STEP):
            for c in copies(first + s, s):
                c.start()

    x = x_ref[...]
    u = _rms(x, g_ref[...]).astype(BF16)
    acc = jnp.zeros(x.shape, F32)
    n_chunks = D_FF // FF_CHUNK
    bounds = [(s * n_chunks) // DEC_PER_STEP for s in range(DEC_PER_STEP + 1)]
    for s in range(DEC_PER_STEP):
        for c in copies(first + s, s):
            c.wait()
        acc = _swiglu_chunks(u, wg_ref, wu_ref, wd_ref, acc, bounds[s], bounds[s + 1])
        decode(s)

        @pl.when(more)
        def _():
            for c in copies(first + DEC_PER_STEP + s, s):
                c.start()

    y = x + 0.5 * acc
    if final:
        y = _rms(y, fg_ref[...])
    y_ref[...] = y


def _ffn_decode(x, layer, g, wg, wu, wd, dec, sample_base, final_g=None, tm=512):
    page_table, lam_rows, subln, q, k_new, v_new4, cache_kt, cache_v, lam_init = dec
    m = x.shape[0]
    steps = m // tm
    n_dec = DEC_PER_STEP * steps
    assert sample_base % DEC_PER_STEP == 0 and sample_base + n_dec <= q.shape[0]
    n_pages = page_table.shape[1]
    page_shape = (DEC_PER_STEP, n_pages, ATTN_WIDTH, LANES)
    const = lambda shape, index: pl.BlockSpec(shape, lambda i, pt: index, pipeline_mode=pl.Buffered(1))
    per_sample = lambda *shape: pl.BlockSpec((DEC_PER_STEP,) + shape,
                                             lambda i, pt: (sample_base // DEC_PER_STEP + i, 0, 0))
    in_specs = [
        pl.BlockSpec((tm, D_MODEL), lambda i, pt: (i, 0)),
        const((None, 1, D_MODEL), (layer, 0, 0)),
        const((None, D_MODEL, D_FF), (layer, 0, 0)),
        const((None, D_MODEL, D_FF), (layer, 0, 0)),
        const((None, D_FF, D_MODEL), (layer, 0, 0)),
    ]
    args = [x, g, wg, wu, wd]
    if final_g is not None:
        in_specs.append(const((1, D_MODEL), (0, 0)))
        args.append(final_g)
    in_specs += [
        const((None, 4, HEAD_DIM), (layer, 0, 0)),
        const((None, 1, LANES), (layer, 0, 0)),
        per_sample(1, ATTN_WIDTH),
        per_sample(1, ATTN_WIDTH),
        per_sample(N_HEADS, LANES),
        pl.BlockSpec(memory_space=pl.ANY),
        pl.BlockSpec(memory_space=pl.ANY),
    ]
    args += [lam_rows, subln, q, k_new, v_new4, cache_kt, cache_v]
    grid_spec = pltpu.PrefetchScalarGridSpec(
        num_scalar_prefetch=1,
        grid=(steps,),
        in_specs=in_specs,
        out_specs=[pl.BlockSpec((tm, D_MODEL), lambda i, pt: (i, 0)),
                   pl.BlockSpec((DEC_PER_STEP, N_HEADS, LANES), lambda i, pt: (i, 0, 0))],
        scratch_shapes=[pltpu.VMEM(page_shape, F32), pltpu.VMEM(page_shape, F32),
                        pltpu.SemaphoreType.DMA((2, DEC_PER_STEP, n_pages))],
    )
    return pl.pallas_call(
        functools.partial(_ffn_decode_body, final=final_g is not None, lam_init=lam_init, layer=layer,
                          sample_base=sample_base),
        grid_spec=grid_spec,
        out_shape=[jax.ShapeDtypeStruct((m, D_MODEL), F32),
                   jax.ShapeDtypeStruct((n_dec, N_HEADS, LANES), BF16)],
        compiler_params=_params("arbitrary"),
        name="ffn_decode",
    )(page_table, *args)


def _cmul(ar, ai, br, bi):
    return ar * br - ai * bi, ar * bi + ai * br


def _ssm_prep_body(are_ref, aim_ref, ldt_ref, btr_ref, bti_ref, ctr_ref, cti_ref,
                   wk_ref, ws_ref, wc_ref, wc0_ref, dec_ref):
    t = T_CHUNK
    a_re = are_ref[...]
    a_im = aim_ref[...]
    dt = jnp.exp(ldt_ref[...])
    mag = jnp.exp(a_re * dt)
    p_re = mag * jnp.cos(a_im * dt)
    p_im = mag * jnp.sin(a_im * dt)
    den = a_re * a_re + a_im * a_im
    nr = p_re - 1.0
    coef_re = (nr * a_re + p_im * a_im) / den
    coef_im = (p_im * a_re - nr * a_im) / den

    r = lax.broadcasted_iota(jnp.int32, (LANES, SG_STATE), 0)
    c = lax.broadcasted_iota(jnp.int32, (LANES, SG_STATE), 1)
    bt_mask = r // GROUP_SIZE == c // STATE_DIM
    r = lax.broadcasted_iota(jnp.int32, (SG_STATE, LANES), 0)
    c = lax.broadcasted_iota(jnp.int32, (SG_STATE, LANES), 1)
    ct_mask = r // STATE_DIM == c // GROUP_SIZE
    btr = jnp.where(bt_mask, btr_ref[...], 0.0)
    bti = jnp.where(bt_mask, bti_ref[...], 0.0)
    ctr = jnp.where(ct_mask, ctr_ref[...], 0.0)
    cti = jnp.where(ct_mask, cti_ref[...], 0.0)
    bbr, bbi = _cmul(coef_re, coef_im, btr, bti)

    pow_re = [jnp.ones_like(p_re)]
    pow_im = [jnp.zeros_like(p_im)]
    col = lambda x: jnp.transpose(jnp.broadcast_to(x, (LANES, SG_STATE)))
    p_re_c, p_im_c = col(p_re), col(p_im)
    cpow_re = [jnp.ones_like(p_re_c)]
    cpow_im = [jnp.zeros_like(p_im_c)]
    for _ in range(t):
        nr_, ni_ = _cmul(pow_re[-1], pow_im[-1], p_re, p_im)
        pow_re.append(nr_)
        pow_im.append(ni_)
        nr_, ni_ = _cmul(cpow_re[-1], cpow_im[-1], p_re_c, p_im_c)
        cpow_re.append(nr_)
        cpow_im.append(ni_)

    ct_cat = jnp.concatenate([ctr, cti], axis=0)
    wk_ref[...] = jnp.zeros(wk_ref.shape, wk_ref.dtype)
    for j in range(t):
        er, ei = _cmul(bbr, bbi, pow_re[j], pow_im[j])
        kj = jnp.dot(jnp.concatenate([er, -ei], axis=1), ct_cat, precision=_HI,
                     preferred_element_type=F32).astype(BF16)
        for t_in in range(t - j):
            wk_ref[t_in * LANES:(t_in + 1) * LANES, (t_in + j) * LANES:(t_in + j + 1) * LANES] = kj
        rows = slice((t - 1 - j) * LANES, (t - j) * LANES)
        ws_ref[rows, :SG_STATE] = er.astype(BF16)
        ws_ref[rows, SG_STATE:] = ei.astype(BF16)
    for j in range(1, t + 1):
        cols = slice((j - 1) * LANES, j * LANES)
        wc_ref[:SG_STATE, cols] = (ctr * cpow_re[j] - cti * cpow_im[j]).astype(BF16)
        wc_ref[SG_STATE:, cols] = (-(ctr * cpow_im[j] + cti * cpow_re[j])).astype(BF16)
    wc0_ref[:SG_STATE, :] = ctr.astype(BF16)
    wc0_ref[SG_STATE:, :] = (-cti).astype(BF16)

    dec_rows = [jnp.concatenate([p_re, p_im], axis=1)]
    dr, di = pow_re[t], pow_im[t]
    for _ in range(N_DEC_ROWS - 1):
        dec_rows.append(jnp.concatenate([dr, di], axis=1))
        dr, di = _cmul(dr, di, dr, di)
    dec_ref[...] = jnp.concatenate(dec_rows, axis=0)


def _ssm_prep(a_re, a_im, log_dt, b_re, b_im, c_re, c_im):
    depth = a_re.shape[0]
    rowed = lambda x: x.reshape(depth, N_SG, 1, SG_STATE)
    ldt = jnp.broadcast_to(log_dt[:, :, None], (depth, N_GROUPS, STATE_DIM))

    def bt(x):
        y = x.reshape(depth, N_SG, SG_GROUPS, STATE_DIM, GROUP_SIZE).transpose(0, 1, 4, 2, 3)
        y = y.reshape(depth, N_SG, 1, GROUP_SIZE, SG_STATE)
        return jnp.broadcast_to(y, (depth, N_SG, SG_GROUPS, GROUP_SIZE, SG_STATE)).reshape(
            depth, N_SG, LANES, SG_STATE)

    def ct(x):
        y = x.reshape(depth, N_SG, SG_GROUPS, GROUP_SIZE, STATE_DIM).transpose(0, 1, 4, 2, 3)
        y = y.reshape(depth, N_SG, 1, STATE_DIM, LANES)
        return jnp.broadcast_to(y, (depth, N_SG, SG_GROUPS, STATE_DIM, LANES)).reshape(
            depth, N_SG, SG_STATE, LANES)

    tk = T_CHUNK * LANES
    blk = lambda *shape: pl.BlockSpec((None, None) + shape, lambda l, g: (l, g, 0, 0))
    return pl.pallas_call(
        _ssm_prep_body,
        grid=(depth, N_SG),
        in_specs=[blk(1, SG_STATE)] * 3 + [blk(LANES, SG_STATE)] * 2 + [blk(SG_STATE, LANES)] * 2,
        out_specs=[blk(tk, tk), blk(tk, 2 * SG_STATE), blk(2 * SG_STATE, tk),
                   blk(2 * SG_STATE, LANES), blk(N_DEC_ROWS, 2 * SG_STATE)],
        out_shape=[jax.ShapeDtypeStruct((depth, N_SG, tk, tk), BF16),
                   jax.ShapeDtypeStruct((depth, N_SG, tk, 2 * SG_STATE), BF16),
                   jax.ShapeDtypeStruct((depth, N_SG, 2 * SG_STATE, tk), BF16),
                   jax.ShapeDtypeStruct((depth, N_SG, 2 * SG_STATE, LANES), BF16),
                   jax.ShapeDtypeStruct((depth, N_SG, N_DEC_ROWS, 2 * SG_STATE), F32)],
        compiler_params=_params("parallel", "parallel"),
        name="ssm_prep",
    )(rowed(a_re), rowed(a_im), rowed(ldt), bt(b_re), bt(b_im), ct(c_re), ct(c_im))


def _ssm_prompt_body(us_ref, wk_ref, ws_ref, wc_ref, dec_ref, y_ref, st_ref):
    t = T_CHUNK
    n_chunks = us_ref.shape[0] // t
    x = jnp.concatenate([us_ref[pl.ds(i, n_chunks, stride=t), :] for i in range(t)], axis=1).astype(BF16)
    h_loc = _dot(x, ws_ref[...])
    s_re = h_loc[:, :SG_STATE]
    s_im = h_loc[:, SG_STATE:]
    row = lax.broadcasted_iota(jnp.int32, s_re.shape, 0)
    shifted = lambda v, d: jnp.where(row >= d, pltpu.roll(v, d, 0), 0.0)
    d = 1
    i = 1
    while d < n_chunks:
        a_r = dec_ref[i:i + 1, :SG_STATE]
        a_i = dec_ref[i:i + 1, SG_STATE:]
        add_re, add_im = _cmul(a_r, a_i, shifted(s_re, d), shifted(s_im, d))
        s_re = s_re + add_re
        s_im = s_im + add_im
        d *= 2
        i += 1
    st_ref[0:1, :] = s_re[n_chunks - 1:n_chunks, :]
    st_ref[1:2, :] = s_im[n_chunks - 1:n_chunks, :]
    h_in = jnp.concatenate([shifted(s_re, 1), shifted(s_im, 1)], axis=1).astype(BF16)
    y = _dot(x, wk_ref[...]) + _dot(h_in, wc_ref[...])
    for i in range(t):
        y_ref[pl.ds(i, n_chunks, stride=t), :] = y[:, i * LANES:(i + 1) * LANES]


def _ssm_prompt(us, layer, mats):
    wk, ws, wc, _, dec = mats
    b, l, _ = us.shape
    assert l // T_CHUNK <= 2 ** (N_DEC_ROWS - 1)
    tk = T_CHUNK * LANES
    mat = lambda r, c: pl.BlockSpec((None, None, r, c), lambda g, i: (layer, g, 0, 0))
    return pl.pallas_call(
        _ssm_prompt_body,
        grid=(N_SG, b),
        in_specs=[pl.BlockSpec((None, l, LANES), lambda g, i: (i, 0, g)),
                  mat(tk, tk), mat(tk, 2 * SG_STATE), mat(2 * SG_STATE, tk),
                  mat(N_DEC_ROWS, 2 * SG_STATE)],
        out_specs=[pl.BlockSpec((None, l, LANES), lambda g, i: (i, 0, g)),
                   pl.BlockSpec((None, None, 2, SG_STATE), lambda g, i: (i, g, 0, 0))],
        out_shape=[jax.ShapeDtypeStruct((b, l, SSM_WIDTH), F32),
                   jax.ShapeDtypeStruct((b, N_SG, 2, SG_STATE), F32)],
        compiler_params=_params("parallel", "parallel"),
        name="ssm_prompt",
    )(us, wk, ws, wc, dec)


def _ssm_sample_body(us_ref, h0r_ref, h0i_ref, ws_ref, wc0_ref, dec_ref, y_ref, hr_ref, hi_ref):
    bu = _dot(us_ref[...].astype(BF16), ws_ref[...])
    a_r = dec_ref[0:1, :SG_STATE]
    a_i = dec_ref[0:1, SG_STATE:]
    dr, di = _cmul(a_r, a_i, h0r_ref[...], h0i_ref[...])
    h_re = bu[:, :SG_STATE] + dr
    h_im = bu[:, SG_STATE:] + di
    hr_ref[...] = h_re
    hi_ref[...] = h_im
    y_ref[...] = _dot(jnp.concatenate([h_re, h_im], axis=1).astype(BF16), wc0_ref[...])


def _ssm_sample(us, h0_re, h0_im, layer, mats):
    _, ws, _, wc0, dec = mats
    b = us.shape[0]
    st_spec = pl.BlockSpec((None, b, SG_STATE), lambda g: (g, 0, 0))
    return pl.pallas_call(
        _ssm_sample_body,
        grid=(N_SG,),
        in_specs=[pl.BlockSpec((b, LANES), lambda g: (0, g)), st_spec, st_spec,
                  pl.BlockSpec((None, None, LANES, 2 * SG_STATE), lambda g: (layer, g, T_CHUNK - 1, 0)),
                  pl.BlockSpec((None, None, 2 * SG_STATE, LANES), lambda g: (layer, g, 0, 0)),
                  pl.BlockSpec((None, None, N_DEC_ROWS, 2 * SG_STATE), lambda g: (layer, g, 0, 0))],
        out_specs=[pl.BlockSpec((b, LANES), lambda g: (0, g)), st_spec, st_spec],
        out_shape=[jax.ShapeDtypeStruct((b, SSM_WIDTH), F32),
                   jax.ShapeDtypeStruct((N_SG, b, SG_STATE), F32),
                   jax.ShapeDtypeStruct((N_SG, b, SG_STATE), F32)],
        compiler_params=_params("parallel"),
        name="ssm_sample",
    )(us, h0_re, h0_im, ws, wc0, dec)


def _gelu_tanh(x):
    return 0.5 * x * (1.0 + jnp.tanh(math.sqrt(2.0 / math.pi) * (x + 0.044715 * (x * x * x))))


def _merge_body(h_ref, o_ref, y_ref, us_ref, g_ref, wgate_ref, bgate_ref, d_ref, gluw_ref, glub_ref,
                wa_ref, ws_ref, wo_ref, out_ref):
    h = h_ref[...]
    u = _rms(h, g_ref[...]).astype(BF16)
    gates = jax.nn.sigmoid(_dot(u, wgate_ref[...]) + bgate_ref[...])
    y = _gelu_tanh(y_ref[...] + d_ref[...] * us_ref[...])
    ys = y * jax.nn.sigmoid(_dot(y.astype(BF16), gluw_ref[...]) + glub_ref[...])
    a = _dot(o_ref[...], wa_ref[...])
    s = _dot(ys.astype(BF16), ws_ref[...])
    mix = gates[:, :D_MODEL] * a + gates[:, D_MODEL:] * s
    out_ref[...] = h + _dot(mix.astype(BF16), wo_ref[...])


def _merge(h, o, y, us, layer, w, tm=512):
    m = h.shape[0]
    tm = min(tm, m)
    row = lambda n: pl.BlockSpec((tm, n), lambda i: (i, 0))
    return pl.pallas_call(
        _merge_body,
        grid=(m // tm,),
        in_specs=[row(D_MODEL), row(ATTN_WIDTH), row(SSM_WIDTH), row(SSM_WIDTH),
                  _const_spec((None, 1, D_MODEL), (layer, 0, 0)),
                  _const_spec((None, D_MODEL, 2 * D_MODEL), (layer, 0, 1)),
                  _const_spec((None, 1, 2 * D_MODEL), (layer, 0, 0)),
                  _const_spec((None, 1, SSM_WIDTH), (layer, 0, 0)),
                  _const_spec((None, SSM_WIDTH, SSM_WIDTH), (layer, 0, 0)),
                  _const_spec((None, 1, SSM_WIDTH), (layer, 0, 0)),
                  _const_spec((None, ATTN_WIDTH, D_MODEL), (layer, 0, 0)),
                  _const_spec((None, SSM_WIDTH, D_MODEL), (layer, 0, 0)),
                  _const_spec((None, D_MODEL, D_MODEL), (layer, 0, 0))],
        out_specs=row(D_MODEL),
        out_shape=jax.ShapeDtypeStruct((m, D_MODEL), F32),
        compiler_params=_params("parallel"),
        name="merge",
    )(h, o, y, us, w["norm_mix"], w["w_in"], w["b_gate"], w["ssm_d"], w["glu_w"], w["glu_b"],
      w["w_attn_out"], w["w_ssm_out"], w["w_out"])


def _rope_tables(pos):
    half = HEAD_DIM // 2
    inv = ROPE_THETA ** (-jnp.arange(half, dtype=F32) / half)
    ang = pos.astype(F32)[:, None] * inv[None, :]
    cos = jnp.cos(ang)
    sin = jnp.sin(ang)
    reps = ATTN_WIDTH // HEAD_DIM
    cos = jnp.tile(jnp.concatenate([cos, cos], axis=1), (1, reps))
    sin = jnp.tile(jnp.concatenate([-sin, sin], axis=1), (1, reps))
    return cos, sin, cos.T, sin.T


def kernel(x_prompt, x_sample, cache_k, cache_v, state_ssm_re, state_ssm_im, page_table, norm_ffn1, ffn1_w_gate, ffn1_w_up, ffn1_w_down, norm_mix, w_in, b_gate, lambda_q1, lambda_k1, lambda_q2, lambda_k2, attn_subln, w_attn_out, ssm_a_re, ssm_a_im, ssm_log_dt, ssm_b_re, ssm_b_im, ssm_c_re, ssm_c_im, ssm_d, glu_w, glu_b, w_ssm_out, w_out, norm_ffn2, ffn2_w_gate, ffn2_w_up, ffn2_w_down, final_norm):
    batch, seq, _ = x_prompt.shape
    dec_batch, dec_seq, _ = x_sample.shape
    depth = w_in.shape[0]
    n_pool = cache_k.shape[1]
    past_len = page_table.shape[1] * PAGE_SIZE
    assert dec_seq == 1

    vec = lambda x: x.reshape(depth, 1, -1)
    bf = lambda x: x.astype(BF16)
    w_in_b = bf(w_in)
    w_kt = bf(jnp.swapaxes(w_in[:, :, ATTN_WIDTH:2 * ATTN_WIDTH], 1, 2))
    mw = dict(norm_mix=vec(norm_mix), w_in=w_in_b, b_gate=vec(b_gate), ssm_d=vec(ssm_d), glu_w=bf(glu_w),
              glu_b=vec(glu_b), w_attn_out=bf(w_attn_out), w_ssm_out=bf(w_ssm_out), w_out=bf(w_out))
    f1 = (vec(norm_ffn1), bf(ffn1_w_gate), bf(ffn1_w_up), bf(ffn1_w_down))
    f2 = (vec(norm_ffn2), bf(ffn2_w_gate), bf(ffn2_w_up), bf(ffn2_w_down))
    lam_rows = jnp.stack([lambda_q1, lambda_k1, lambda_q2, lambda_k2], axis=1)
    subln = vec(attn_subln)
    final_g = final_norm.reshape(1, D_MODEL)

    tabs_p = _rope_tables(jnp.arange(seq, dtype=jnp.int32))
    tabs_s = _rope_tables(jnp.full((dec_batch,), past_len, jnp.int32))
    cache_kt = jnp.transpose(cache_k, (0, 1, 3, 4, 2)).reshape(depth, n_pool, ATTN_WIDTH, PAGE_SIZE)
    cache_v2 = cache_v.reshape(depth, n_pool, PAGE_SIZE * N_HEADS, 2 * HEAD_DIM)

    mats = _ssm_prep(ssm_a_re, ssm_a_im, ssm_log_dt, ssm_b_re, ssm_b_im, ssm_c_re, ssm_c_im)
    sg_state = lambda x: x.reshape(dec_batch, N_SG, SG_STATE).transpose(1, 0, 2)
    sg_unstate = lambda x: x.transpose(1, 0, 2).reshape(dec_batch, N_GROUPS, STATE_DIM)

    xp = x_prompt.reshape(batch * seq, D_MODEL)
    xs = x_sample.reshape(dec_batch, D_MODEL)
    outs = [[] for _ in range(8)]
    kt_all = jnp.zeros((depth, batch, ATTN_WIDTH, seq), F32)
    v4_all = jnp.zeros((depth, batch, seq * N_HEADS, 2 * HEAD_DIM), F32)
    for l in range(depth):
        lam_init = 0.8 - 0.6 * math.exp(-0.3 * l)
        last = l == depth - 1

        hs = _ffn(xs, l, *f1)
        q, kt, _, v4, _, us_s, k_new = _inproj(hs.reshape(1, dec_batch, D_MODEL), l, mw["norm_mix"], w_in_b, w_kt,
                                               tabs_s, k_rows=True)
        v_new4 = v4.reshape(dec_batch, N_HEADS, 2 * HEAD_DIM)
        dec = (page_table, lam_rows, subln, q.reshape(dec_batch, 1, ATTN_WIDTH),
               k_new.reshape(dec_batch, 1, ATTN_WIDTH), v_new4, cache_kt, cache_v2, lam_init)

        h, o_lo = _ffn_decode(xp, l, *f1, dec, 0)
        q, kt_all, ktb, v4_all, vb, us = _inproj(h.reshape(batch, seq, D_MODEL), l, mw["norm_mix"], w_in_b, w_kt,
                                                 tabs_p, stacked=(kt_all, v4_all))
        o = _attn_prompt(q, ktb, vb, l, lam_rows, subln, lam_init)
        y, st = _ssm_prompt(us, l, mats)
        h = _merge(h, o.reshape(batch * seq, ATTN_WIDTH), y.reshape(batch * seq, SSM_WIDTH),
                   us.reshape(batch * seq, SSM_WIDTH), l, mw)
        xp, o_hi = _ffn_decode(h, l, *f2, dec, o_lo.shape[0], final_g=final_g if last else None)
        assert o_lo.shape[0] + o_hi.shape[0] == dec_batch
        outs[2].append(st[:, :, 0, :].reshape(batch, N_GROUPS, STATE_DIM))
        outs[3].append(st[:, :, 1, :].reshape(batch, N_GROUPS, STATE_DIM))

        o = jnp.concatenate([o_lo, o_hi], axis=0)
        us = us_s.reshape(dec_batch, SSM_WIDTH)
        y, h_re, h_im = _ssm_sample(us, sg_state(state_ssm_re[l]), sg_state(state_ssm_im[l]), l, mats)
        h = _merge(hs, o.reshape(dec_batch, ATTN_WIDTH), y, us, l, mw)
        xs = _ffn(h, l, *f2, final_g=final_g if last else None)
        outs[4].append(kt.reshape(2 * N_HEADS, HEAD_DIM, dec_batch).transpose(2, 0, 1)[:, None])
        outs[5].append(v_new4[:, None])
        outs[6].append(sg_unstate(h_re))
        outs[7].append(sg_unstate(h_im))

    stk = [jnp.stack(o) for o in outs[2:]]
    k_prompt = kt_all.reshape(depth, batch, 2 * N_HEADS, HEAD_DIM, seq).transpose(0, 1, 4, 2, 3)
    v_prompt = v4_all.reshape(depth, batch, seq, N_HEADS, 2 * HEAD_DIM)
    return (xp.reshape(batch, seq, D_MODEL), xs.reshape(dec_batch, dec_seq, D_MODEL),
            k_prompt, v_prompt, stk[0], stk[1], stk[2], stk[3], stk[4], stk[5])
```

```python
import functools
import math

import jax
import jax.numpy as jnp
from jax import lax
from jax.experimental import pallas as pl
from jax.experimental.pallas import tpu as pltpu

F32 = jnp.float32
BF16 = jnp.bfloat16

D_MODEL = 1024
N_HEADS = 4
HEAD_DIM = 64
ATTN_WIDTH = 2 * N_HEADS * HEAD_DIM
SSM_WIDTH = 512
GROUP_SIZE = 16
N_GROUPS = SSM_WIDTH // GROUP_SIZE
STATE_DIM = 64
D_FF = 2816
ROPE_THETA = 10000.0
NORM_EPS = 1e-6
PAGE_SIZE = 128

LANES = 128
SG_GROUPS = LANES // GROUP_SIZE
N_SG = N_GROUPS // SG_GROUPS
SG_STATE = SG_GROUPS * STATE_DIM
T_CHUNK = 8
N_DEC_ROWS = 16
FF_CHUNK = 256
VMEM_LIMIT = 56 * 2**20

_HI = lax.Precision.HIGHEST


def _dot(a, b):
    return jnp.dot(a, b, preferred_element_type=F32)


def _rms(x, g):
    return x * lax.rsqrt(jnp.mean(x * x, axis=-1, keepdims=True) + NORM_EPS) * g


def _tree(op, xs):
    xs = list(xs)
    while len(xs) > 1:
        xs = [op(xs[k], xs[k + 1]) if k + 1 < len(xs) else xs[k] for k in range(0, len(xs), 2)]
    return xs[0]


def _params(*sem):
    return pltpu.CompilerParams(dimension_semantics=sem, vmem_limit_bytes=VMEM_LIMIT)


def _const_spec(shape, index):
    return pl.BlockSpec(shape, lambda *_: index, pipeline_mode=pl.Buffered(1))


def _swiglu_chunks(u, wg_ref, wu_ref, wd_ref, acc, first, last):
    for c in range(first, last):
        cols = slice(c * FF_CHUNK, (c + 1) * FF_CHUNK)
        gate = _dot(u, wg_ref[:, cols])
        up = _dot(u, wu_ref[:, cols])
        act = (gate * jax.nn.sigmoid(gate) * up).astype(BF16)
        acc = acc + _dot(act, wd_ref[cols, :])
    return acc


def _ffn_body(*refs, final):
    if final:
        x_ref, g_ref, wg_ref, wu_ref, wd_ref, fg_ref, o_ref = refs
    else:
        x_ref, g_ref, wg_ref, wu_ref, wd_ref, o_ref = refs
    x = x_ref[...]
    u = _rms(x, g_ref[...]).astype(BF16)
    acc = _swiglu_chunks(u, wg_ref, wu_ref, wd_ref, jnp.zeros(x.shape, F32), 0, D_FF // FF_CHUNK)
    y = x + 0.5 * acc
    if final:
        y = _rms(y, fg_ref[...])
    o_ref[...] = y


def _ffn(x, layer, g, wg, wu, wd, final_g=None, tm=512):
    m = x.shape[0]
    tm = min(tm, m)
    in_specs = [
        pl.BlockSpec((tm, D_MODEL), lambda i: (i, 0)),
        _const_spec((None, 1, D_MODEL), (layer, 0, 0)),
        _const_spec((None, D_MODEL, D_FF), (layer, 0, 0)),
        _const_spec((None, D_MODEL, D_FF), (layer, 0, 0)),
        _const_spec((None, D_FF, D_MODEL), (layer, 0, 0)),
    ]
    args = [x, g, wg, wu, wd]
    if final_g is not None:
        in_specs.append(_const_spec((1, D_MODEL), (0, 0)))
        args.append(final_g)
    return pl.pallas_call(
        functools.partial(_ffn_body, final=final_g is not None),
        grid=(m // tm,),
        in_specs=in_specs,
        out_specs=pl.BlockSpec((tm, D_MODEL), lambda i: (i, 0)),
        out_shape=jax.ShapeDtypeStruct((m, D_MODEL), F32),
        compiler_params=_params("parallel"),
        name="ffn",
    )(*args)


def _swap_halves_lanes(x):
    n = x.shape[-1]
    half = HEAD_DIM // 2
    lane = lax.broadcasted_iota(jnp.int32, x.shape, x.ndim - 1)
    return jnp.where(lane % HEAD_DIM < half, pltpu.roll(x, n - half, x.ndim - 1),
                     pltpu.roll(x, half, x.ndim - 1))


def _swap_halves_rows(x):
    half = HEAD_DIM // 2
    parts = []
    for r in range(0, x.shape[0], HEAD_DIM):
        parts += [x[r + half:r + HEAD_DIM], x[r:r + half]]
    return jnp.concatenate(parts, axis=0)


def _inproj_body(*refs, k_rows, n_aliased):
    n_in = 10 + (1 if k_rows else 0) + n_aliased
    x_ref, g_ref, wq_ref, wkt_ref, wv_ref, wus_ref, cos_ref, sin_ref, cost_ref, sint_ref = refs[:10]
    q_ref, kt_ref, ktb_ref, v4_ref, vb_ref, us_ref = refs[n_in:n_in + 6]
    tm = x_ref.shape[0]
    u = _rms(x_ref[...], g_ref[...]).astype(BF16)
    cos = cos_ref[...]
    sin = sin_ref[...]
    q = _dot(u, wq_ref[...])
    q = q * cos + _swap_halves_lanes(q) * sin
    q_ref[...] = (q * (HEAD_DIM ** -0.5)).astype(BF16)
    kt = lax.dot_general(wkt_ref[...], u, (((1,), (1,)), ((), ())), preferred_element_type=F32)
    kt = kt * cost_ref[...] + _swap_halves_rows(kt) * sint_ref[...]
    kt_ref[...] = kt
    ktb_ref[...] = kt.astype(BF16)
    v = _dot(u, wv_ref[...])
    vb_ref[...] = v.astype(BF16)
    for h in range(N_HEADS):
        v4_ref[pl.ds(h, tm, stride=N_HEADS), :] = v[:, h * LANES:(h + 1) * LANES]
    us_ref[...] = _dot(u, wus_ref[...])
    if k_rows:
        wk_ref, k_ref = refs[10], refs[n_in + 6]
        k = _dot(u, wk_ref[...])
        k_ref[...] = k * cos + _swap_halves_lanes(k) * sin


def _inproj(x, layer, g, w_in, w_kt, tabs, k_rows=False, stacked=None, tm=512):
    b, l, _ = x.shape
    tm = min(tm, l)
    cos, sin, cost, sint = tabs
    aw = ATTN_WIDTH
    wcol = lambda c: _const_spec((None, D_MODEL, aw), (layer, 0, c))
    in_specs = [
        pl.BlockSpec((None, tm, D_MODEL), lambda j, i: (i, j, 0)),
        _const_spec((None, 1, D_MODEL), (layer, 0, 0)),
        wcol(0),
        _const_spec((None, aw, D_MODEL), (layer, 0, 0)),
        wcol(2),
        wcol(3),
        pl.BlockSpec((tm, aw), lambda j, i: (j, 0)),
        pl.BlockSpec((tm, aw), lambda j, i: (j, 0)),
        pl.BlockSpec((aw, tm), lambda j, i: (0, j)),
        pl.BlockSpec((aw, tm), lambda j, i: (0, j)),
    ]
    args = [x, g, w_in, w_kt, w_in, w_in, cos, sin, cost, sint]
    row_spec = pl.BlockSpec((None, tm, aw), lambda j, i: (i, j, 0))
    col_spec = pl.BlockSpec((None, aw, tm), lambda j, i: (i, 0, j))
    out_specs = [row_spec, col_spec, col_spec,
                 pl.BlockSpec((None, tm * N_HEADS, LANES), lambda j, i: (i, j, 0)),
                 row_spec, row_spec]
    out_shape = [jax.ShapeDtypeStruct((b, l, aw), BF16),
                 jax.ShapeDtypeStruct((b, aw, l), F32),
                 jax.ShapeDtypeStruct((b, aw, l), BF16),
                 jax.ShapeDtypeStruct((b, l * N_HEADS, LANES), F32),
                 jax.ShapeDtypeStruct((b, l, aw), BF16),
                 jax.ShapeDtypeStruct((b, l, SSM_WIDTH), F32)]
    if k_rows:
        in_specs.append(wcol(1))
        args.append(w_in)
        out_specs.append(row_spec)
        out_shape.append(jax.ShapeDtypeStruct((b, l, aw), F32))
    aliases = {}
    if stacked is not None:
        for buf, out_idx in zip(stacked, (1, 3)):
            aliases[len(args)] = out_idx
            in_specs.append(pl.BlockSpec(memory_space=pl.ANY))
            args.append(buf)
            out_shape[out_idx] = jax.ShapeDtypeStruct(buf.shape, buf.dtype)
        out_specs[1] = pl.BlockSpec((None, None, aw, tm), lambda j, i: (layer, i, 0, j))
        out_specs[3] = pl.BlockSpec((None, None, tm * N_HEADS, LANES), lambda j, i: (layer, i, j, 0))
    return pl.pallas_call(
        functools.partial(_inproj_body, k_rows=k_rows, n_aliased=len(aliases)),
        grid=(l // tm, b),
        in_specs=in_specs,
        out_specs=out_specs,
        out_shape=out_shape,
        input_output_aliases=aliases,
        compiler_params=_params("parallel", "parallel"),
        name="in_proj",
    )(*args)


def _lambda(lam_ref, lam_init):
    a = jnp.sum(lam_ref[0:1, :] * lam_ref[1:2, :], axis=-1, keepdims=True)
    b = jnp.sum(lam_ref[2:3, :] * lam_ref[3:4, :], axis=-1, keepdims=True)
    return jnp.exp(a) - jnp.exp(b) + lam_init


def _head_norm(o, subln, lam_init):
    return _rms(o, subln) * (1.0 - lam_init)


def _attn_body(lam_ref, subln_ref, q_ref, kt_ref, v_ref, o_ref, s_scr, *, lam_init, tq, tk):
    n_tiles = q_ref.shape[0] // tq
    lam = _lambda(lam_ref, lam_init)
    subln = subln_ref[...]
    lane = lax.broadcasted_iota(jnp.int32, (tq, LANES), 1)
    row = lax.broadcasted_iota(jnp.int32, (tq, tk), 0)
    col = lax.broadcasted_iota(jnp.int32, (tq, tk), 1)

    def one_map(i, which, qm):
        n_kv = -(-(i + 1) * tq // tk)
        m_tile = jnp.full((tq, LANES), -jnp.inf, F32)
        for j in range(n_kv):
            s = _dot(qm, kt_ref[:, j * tk:(j + 1) * tk])
            if j == n_kv - 1:
                s = jnp.where(col + j * tk <= row + i * tq, s, -jnp.inf)
            s_scr[which, :, j * tk:(j + 1) * tk] = s
            for c in range(tk // LANES):
                m_tile = jnp.maximum(m_tile, s[:, c * LANES:(c + 1) * LANES])
        m_b = jnp.broadcast_to(jnp.max(m_tile, axis=-1, keepdims=True), (tq, LANES))
        l_tile = jnp.zeros((tq, LANES), F32)
        acc = jnp.zeros((tq, LANES), F32)
        for j in range(n_kv):
            parts = []
            for c in range(tk // LANES):
                p = jnp.exp(s_scr[which, :, j * tk + c * LANES:j * tk + (c + 1) * LANES] - m_b)
                l_tile = l_tile + p
                parts.append(p.astype(BF16))
            acc = acc + _dot(jnp.concatenate(parts, axis=1), v_ref[j * tk:(j + 1) * tk, :])
        return acc / jnp.sum(l_tile, axis=-1, keepdims=True)

    for i in range(n_tiles):
        rows = slice(i * tq, (i + 1) * tq)
        q = q_ref[rows, :]
        zero = jnp.zeros_like(q)
        o1 = one_map(i, 0, jnp.where(lane < HEAD_DIM, q, zero))
        o2 = one_map(i, 1, jnp.where(lane >= HEAD_DIM, q, zero))
        o_ref[rows, :] = _head_norm(o1 - lam * o2, subln, lam_init).astype(o_ref.dtype)


def _attn_prompt(q, ktb, vb, layer, lam_rows, subln, lam_init, tq=256, tk=256):
    b, l, _ = q.shape
    assert tk % tq == 0 and l % tk == 0
    rows = pl.BlockSpec((None, l, LANES), lambda i, h: (i, 0, h))
    return pl.pallas_call(
        functools.partial(_attn_body, lam_init=lam_init, tq=tq, tk=tk),
        grid=(b, N_HEADS),
        in_specs=[
            _const_spec((None, 4, HEAD_DIM), (layer, 0, 0)),
            _const_spec((None, 1, LANES), (layer, 0, 0)),
            rows,
            pl.BlockSpec((None, LANES, l), lambda i, h: (i, h, 0)),
            rows,
        ],
        out_specs=rows,
        out_shape=jax.ShapeDtypeStruct((b, l, ATTN_WIDTH), BF16),
        scratch_shapes=[pltpu.VMEM((2, tq, l), F32)],
        compiler_params=_params("parallel", "parallel"),
        name="attn_prompt",
    )(lam_rows, subln, q, ktb, vb)


def _rows_to_row_sums(tiles):
    row = lax.broadcasted_iota(jnp.int32, tiles[0].shape, 0)
    step = 4
    while len(tiles) > 1:
        half = len(tiles) // 2
        low = (row & step) == 0
        nxt = []
        for j in range(half):
            a, b = tiles[j], tiles[j + half]
            keep = jnp.where(low, a, b)
            other = jnp.where(low, b, a)
            partner = jnp.where(low, pltpu.roll(other, 8 - step, 0), pltpu.roll(other, step, 0))
            nxt.append(keep + partner)
        tiles = nxt
        step //= 2
    return tiles[0]


def _decode_one(q, k_new, v_new, k_refs, v_refs, lam, subln, lam_init):
    n_sub = 2 * N_HEADS
    sub_of_row = [2 * r if r < N_HEADS else 2 * (r - N_HEADS) + 1 for r in range(n_sub)]
    qf = q.astype(F32)
    row = lax.broadcasted_iota(jnp.int32, (n_sub, ATTN_WIDTH), 0)
    lane = lax.broadcasted_iota(jnp.int32, (n_sub, ATTN_WIDTH), 1)
    sub = jnp.where(row < N_HEADS, 2 * row, 2 * (row - N_HEADS) + 1)
    q_blk = jnp.where(lane // HEAD_DIM == sub, qf, 0.0)
    s_new = jnp.sum(q_blk * k_new, axis=-1, keepdims=True)
    q_col = jnp.transpose(jnp.broadcast_to(qf, (LANES, ATTN_WIDTH)))
    scores = []
    for k_ref in k_refs:
        prod = k_ref[...] * q_col
        tiles = [jnp.sum(prod[s * HEAD_DIM:(s + 1) * HEAD_DIM].reshape(HEAD_DIM // 8, 8, LANES), axis=0)
                 for s in sub_of_row]
        scores.append(_rows_to_row_sums(tiles))
    m = jnp.maximum(s_new, jnp.max(_tree(jnp.maximum, scores), axis=-1, keepdims=True))
    p_new = jnp.exp(s_new - m)
    probs = [jnp.exp(s - m) for s in scores]
    inv = 1.0 / (p_new + jnp.sum(_tree(jnp.add, probs), axis=-1, keepdims=True))
    r8 = lax.broadcasted_iota(jnp.int32, (n_sub, 1), 0)
    coef = jnp.where(r8 < N_HEADS, inv, -lam * inv)

    def fold(x):
        y = x * coef
        return y + pltpu.roll(y, N_HEADS, 0)

    tiles_per_page = PAGE_SIZE * N_HEADS // 8
    j3 = lax.broadcasted_iota(jnp.int32, (tiles_per_page, 8, LANES), 0)
    r3 = lax.broadcasted_iota(jnp.int32, (tiles_per_page, 8, LANES), 1)
    l3 = lax.broadcasted_iota(jnp.int32, (tiles_per_page, 8, LANES), 2)
    pick = l3 == 2 * j3 + r3 // N_HEADS
    terms = []
    for p, v_ref in zip(probs, v_refs):
        w = jnp.sum(jnp.where(pick, fold(p)[None], 0.0), axis=-1, keepdims=True)
        terms.append(jnp.sum(w * v_ref[...].reshape(tiles_per_page, 8, LANES), axis=0))
    acc = _tree(jnp.add, terms)
    w_new = fold(jnp.broadcast_to(p_new, (n_sub, LANES)))
    o = acc[:N_HEADS] + acc[N_HEADS:] + w_new[:N_HEADS] * v_new
    return _head_norm(o, subln, lam_init)


def _page_copies(pt_ref, ckt_hbm, cv_hbm, kbuf, vbuf, sems, layer, sample, slot):
    copies = []
    for p in range(kbuf.shape[1]):
        page = pt_ref[sample, p]
        copies.append(pltpu.make_async_copy(ckt_hbm.at[layer, page], kbuf.at[slot, p], sems.at[0, slot, p]))
        copies.append(pltpu.make_async_copy(cv_hbm.at[layer, page], vbuf.at[slot, p], sems.at[1, slot, p]))
    return copies


DEC_PER_STEP = 2


def _ffn_decode_body(*refs, final, lam_init, layer, sample_base):
    n_in = 14 if final else 13
    pt_ref, x_ref, g_ref, wg_ref, wu_ref, wd_ref = refs[:6]
    fg_ref = refs[6] if final else None
    lam_ref, subln_ref, q_ref, knew_ref, vnew_ref, ckt_hbm, cv_hbm = refs[n_in - 7:n_in]
    y_ref, od_ref, kbuf, vbuf, sems = refs[n_in:]
    n_pages = kbuf.shape[1]
    i = pl.program_id(0)
    more = i + 1 < pl.num_programs(0)
    first = sample_base + DEC_PER_STEP * i
    copies = functools.partial(_page_copies, pt_ref, ckt_hbm, cv_hbm, kbuf, vbuf, sems, layer)
    lam = _lambda(lam_ref, lam_init)
    subln = subln_ref[...]

    def decode(s):
        k_refs = [kbuf.at[s, p] for p in range(n_pages)]
        v_refs = [vbuf.at[s, p] for p in range(n_pages)]
        o = _decode_one(q_ref[s], knew_ref[s], vnew_ref[s], k_refs, v_refs, lam, subln, lam_init)
        od_ref[s] = o.astype(od_ref.dtype)

    @pl.when(i == 0)
    def _():
        for s in range(DEC_PER_STEP):
            for c in copies(first + s, s):
                c.start()

    x = x_ref[...]
    u = _rms(x, g_ref[...]).astype(BF16)
    acc = jnp.zeros(x.shape, F32)
    n_chunks = D_FF // FF_CHUNK
    bounds = [(s * n_chunks) // DEC_PER_STEP for s in range(DEC_PER_STEP + 1)]
    for s in range(DEC_PER_STEP):
        for c in copies(first + s, s):
            c.wait()
        acc = _swiglu_chunks(u, wg_ref, wu_ref, wd_ref, acc, bounds[s], bounds[s + 1])
        decode(s)

        @pl.when(more)
        def _():
            for c in copies(first + DEC_PER_STEP + s, s):
                c.start()

    y = x + 0.5 * acc
    if final:
        y = _rms(y, fg_ref[...])
    y_ref[...] = y


def _ffn_decode(x, layer, g, wg, wu, wd, dec, sample_base, final_g=None, tm=512):
    page_table, lam_rows, subln, q, k_new, v_new4, cache_kt, cache_v, lam_init = dec
    m = x.shape[0]
    steps = m // tm
    n_dec = DEC_PER_STEP * steps
    assert sample_base % DEC_PER_STEP == 0 and sample_base + n_dec <= q.shape[0]
    n_pages = page_table.shape[1]
    page_shape = (DEC_PER_STEP, n_pages, ATTN_WIDTH, LANES)
    const = lambda shape, index: pl.BlockSpec(shape, lambda i, pt: index, pipeline_mode=pl.Buffered(1))
    per_sample = lambda *shape: pl.BlockSpec((DEC_PER_STEP,) + shape,
                                             lambda i, pt: (sample_base // DEC_PER_STEP + i, 0, 0))
    in_specs = [
        pl.BlockSpec((tm, D_MODEL), lambda i, pt: (i, 0)),
        const((None, 1, D_MODEL), (layer, 0, 0)),
        const((None, D_MODEL, D_FF), (layer, 0, 0)),
        const((None, D_MODEL, D_FF), (layer, 0, 0)),
        const((None, D_FF, D_MODEL), (layer, 0, 0)),
    ]
    args = [x, g, wg, wu, wd]
    if final_g is not None:
        in_specs.append(const((1, D_MODEL), (0, 0)))
        args.append(final_g)
    in_specs += [
        const((None, 4, HEAD_DIM), (layer, 0, 0)),
        const((None, 1, LANES), (layer, 0, 0)),
        per_sample(1, ATTN_WIDTH),
        per_sample(1, ATTN_WIDTH),
        per_sample(N_HEADS, LANES),
        pl.BlockSpec(memory_space=pl.ANY),
        pl.BlockSpec(memory_space=pl.ANY),
    ]
    args += [lam_rows, subln, q, k_new, v_new4, cache_kt, cache_v]
    grid_spec = pltpu.PrefetchScalarGridSpec(
        num_scalar_prefetch=1,
        grid=(steps,),
        in_specs=in_specs,
        out_specs=[pl.BlockSpec((tm, D_MODEL), lambda i, pt: (i, 0)),
                   pl.BlockSpec((DEC_PER_STEP, N_HEADS, LANES), lambda i, pt: (i, 0, 0))],
        scratch_shapes=[pltpu.VMEM(page_shape, F32), pltpu.VMEM(page_shape, F32),
                        pltpu.SemaphoreType.DMA((2, DEC_PER_STEP, n_pages))],
    )
    return pl.pallas_call(
        functools.partial(_ffn_decode_body, final=final_g is not None, lam_init=lam_init, layer=layer,
                          sample_base=sample_base),
        grid_spec=grid_spec,
        out_shape=[jax.ShapeDtypeStruct((m, D_MODEL), F32),
                   jax.ShapeDtypeStruct((n_dec, N_HEADS, LANES), BF16)],
        compiler_params=_params("arbitrary"),
        name="ffn_decode",
    )(page_table, *args)


def _cmul(ar, ai, br, bi):
    return ar * br - ai * bi, ar * bi + ai * br


def _ssm_prep_body(are_ref, aim_ref, ldt_ref, btr_ref, bti_ref, ctr_ref, cti_ref,
                   wk_ref, ws_ref, wc_ref, wc0_ref, dec_ref):
    t = T_CHUNK
    a_re = are_ref[...]
    a_im = aim_ref[...]
    dt = jnp.exp(ldt_ref[...])
    mag = jnp.exp(a_re * dt)
    p_re = mag * jnp.cos(a_im * dt)
    p_im = mag * jnp.sin(a_im * dt)
    den = a_re * a_re + a_im * a_im
    nr = p_re - 1.0
    coef_re = (nr * a_re + p_im * a_im) / den
    coef_im = (p_im * a_re - nr * a_im) / den

    r = lax.broadcasted_iota(jnp.int32, (LANES, SG_STATE), 0)
    c = lax.broadcasted_iota(jnp.int32, (LANES, SG_STATE), 1)
    bt_mask = r // GROUP_SIZE == c // STATE_DIM
    r = lax.broadcasted_iota(jnp.int32, (SG_STATE, LANES), 0)
    c = lax.broadcasted_iota(jnp.int32, (SG_STATE, LANES), 1)
    ct_mask = r // STATE_DIM == c // GROUP_SIZE
    btr = jnp.where(bt_mask, btr_ref[...], 0.0)
    bti = jnp.where(bt_mask, bti_ref[...], 0.0)
    ctr = jnp.where(ct_mask, ctr_ref[...], 0.0)
    cti = jnp.where(ct_mask, cti_ref[...], 0.0)
    bbr, bbi = _cmul(coef_re, coef_im, btr, bti)

    pow_re = [jnp.ones_like(p_re)]
    pow_im = [jnp.zeros_like(p_im)]
    col = lambda x: jnp.transpose(jnp.broadcast_to(x, (LANES, SG_STATE)))
    p_re_c, p_im_c = col(p_re), col(p_im)
    cpow_re = [jnp.ones_like(p_re_c)]
    cpow_im = [jnp.zeros_like(p_im_c)]
    for _ in range(t):
        nr_, ni_ = _cmul(pow_re[-1], pow_im[-1], p_re, p_im)
        pow_re.append(nr_)
        pow_im.append(ni_)
        nr_, ni_ = _cmul(cpow_re[-1], cpow_im[-1], p_re_c, p_im_c)
        cpow_re.append(nr_)
        cpow_im.append(ni_)

    ct_cat = jnp.concatenate([ctr, cti], axis=0)
    wk_ref[...] = jnp.zeros(wk_ref.shape, wk_ref.dtype)
    for j in range(t):
        er, ei = _cmul(bbr, bbi, pow_re[j], pow_im[j])
        kj = jnp.dot(jnp.concatenate([er, -ei], axis=1), ct_cat, precision=_HI,
                     preferred_element_type=F32).astype(BF16)
        for t_in in range(t - j):
            wk_ref[t_in * LANES:(t_in + 1) * LANES, (t_in + j) * LANES:(t_in + j + 1) * LANES] = kj
        rows = slice((t - 1 - j) * LANES, (t - j) * LANES)
        ws_ref[rows, :SG_STATE] = er.astype(BF16)
        ws_ref[rows, SG_STATE:] = ei.astype(BF16)
    for j in range(1, t + 1):
        cols = slice((j - 1) * LANES, j * LANES)
        wc_ref[:SG_STATE, cols] = (ctr * cpow_re[j] - cti * cpow_im[j]).astype(BF16)
        wc_ref[SG_STATE:, cols] = (-(ctr * cpow_im[j] + cti * cpow_re[j])).astype(BF16)
    wc0_ref[:SG_STATE, :] = ctr.astype(BF16)
    wc0_ref[SG_STATE:, :] = (-cti).astype(BF16)

    dec_rows = [jnp.concatenate([p_re, p_im], axis=1)]
    dr, di = pow_re[t], pow_im[t]
    for _ in range(N_DEC_ROWS - 1):
        dec_rows.append(jnp.concatenate([dr, di], axis=1))
        dr, di = _cmul(dr, di, dr, di)
    dec_ref[...] = jnp.concatenate(dec_rows, axis=0)


def _ssm_prep(a_re, a_im, log_dt, b_re, b_im, c_re, c_im):
    depth = a_re.shape[0]
    rowed = lambda x: x.reshape(depth, N_SG, 1, SG_STATE)
    ldt = jnp.broadcast_to(log_dt[:, :, None], (depth, N_GROUPS, STATE_DIM))

    def bt(x):
        y = x.reshape(depth, N_SG, SG_GROUPS, STATE_DIM, GROUP_SIZE).transpose(0, 1, 4, 2, 3)
        y = y.reshape(depth, N_SG, 1, GROUP_SIZE, SG_STATE)
        return jnp.broadcast_to(y, (depth, N_SG, SG_GROUPS, GROUP_SIZE, SG_STATE)).reshape(
            depth, N_SG, LANES, SG_STATE)

    def ct(x):
        y = x.reshape(depth, N_SG, SG_GROUPS, GROUP_SIZE, STATE_DIM).transpose(0, 1, 4, 2, 3)
        y = y.reshape(depth, N_SG, 1, STATE_DIM, LANES)
        return jnp.broadcast_to(y, (depth, N_SG, SG_GROUPS, STATE_DIM, LANES)).reshape(
            depth, N_SG, SG_STATE, LANES)

    tk = T_CHUNK * LANES
    blk = lambda *shape: pl.BlockSpec((None, None) + shape, lambda l, g: (l, g, 0, 0))
    return pl.pallas_call(
        _ssm_prep_body,
        grid=(depth, N_SG),
        in_specs=[blk(1, SG_STATE)] * 3 + [blk(LANES, SG_STATE)] * 2 + [blk(SG_STATE, LANES)] * 2,
        out_specs=[blk(tk, tk), blk(tk, 2 * SG_STATE), blk(2 * SG_STATE, tk),
                   blk(2 * SG_STATE, LANES), blk(N_DEC_ROWS, 2 * SG_STATE)],
        out_shape=[jax.ShapeDtypeStruct((depth, N_SG, tk, tk), BF16),
                   jax.ShapeDtypeStruct((depth, N_SG, tk, 2 * SG_STATE), BF16),
                   jax.ShapeDtypeStruct((depth, N_SG, 2 * SG_STATE, tk), BF16),
                   jax.ShapeDtypeStruct((depth, N_SG, 2 * SG_STATE, LANES), BF16),
                   jax.ShapeDtypeStruct((depth, N_SG, N_DEC_ROWS, 2 * SG_STATE), F32)],
        compiler_params=_params("parallel", "parallel"),
        name="ssm_prep",
    )(rowed(a_re), rowed(a_im), rowed(ldt), bt(b_re), bt(b_im), ct(c_re), ct(c_im))


def _ssm_prompt_body(us_ref, wk_ref, ws_ref, wc_ref, dec_ref, y_ref, st_ref):
    t = T_CHUNK
    n_chunks = us_ref.shape[0] // t
    x = jnp.concatenate([us_ref[pl.ds(i, n_chunks, stride=t), :] for i in range(t)], axis=1).astype(BF16)
    h_loc = _dot(x, ws_ref[...])
    s_re = h_loc[:, :SG_STATE]
    s_im = h_loc[:, SG_STATE:]
    row = lax.broadcasted_iota(jnp.int32, s_re.shape, 0)
    shifted = lambda v, d: jnp.where(row >= d, pltpu.roll(v, d, 0), 0.0)
    d = 1
    i = 1
    while d < n_chunks:
        a_r = dec_ref[i:i + 1, :SG_STATE]
        a_i = dec_ref[i:i + 1, SG_STATE:]
        add_re, add_im = _cmul(a_r, a_i, shifted(s_re, d), shifted(s_im, d))
        s_re = s_re + add_re
        s_im = s_im + add_im
        d *= 2
        i += 1
    st_ref[0:1, :] = s_re[n_chunks - 1:n_chunks, :]
    st_ref[1:2, :] = s_im[n_chunks - 1:n_chunks, :]
    h_in = jnp.concatenate([shifted(s_re, 1), shifted(s_im, 1)], axis=1).astype(BF16)
    y = _dot(x, wk_ref[...]) + _dot(h_in, wc_ref[...])
    for i in range(t):
        y_ref[pl.ds(i, n_chunks, stride=t), :] = y[:, i * LANES:(i + 1) * LANES]


def _ssm_prompt(us, layer, mats):
    wk, ws, wc, _, dec = mats
    b, l, _ = us.shape
    assert l // T_CHUNK <= 2 ** (N_DEC_ROWS - 1)
    tk = T_CHUNK * LANES
    mat = lambda r, c: pl.BlockSpec((None, None, r, c), lambda g, i: (layer, g, 0, 0))
    return pl.pallas_call(
        _ssm_prompt_body,
        grid=(N_SG, b),
        in_specs=[pl.BlockSpec((None, l, LANES), lambda g, i: (i, 0, g)),
                  mat(tk, tk), mat(tk, 2 * SG_STATE), mat(2 * SG_STATE, tk),
                  mat(N_DEC_ROWS, 2 * SG_STATE)],
        out_specs=[pl.BlockSpec((None, l, LANES), lambda g, i: (i, 0, g)),
                   pl.BlockSpec((None, None, 2, SG_STATE), lambda g, i: (i, g, 0, 0))],
        out_shape=[jax.ShapeDtypeStruct((b, l, SSM_WIDTH), F32),
                   jax.ShapeDtypeStruct((b, N_SG, 2, SG_STATE), F32)],
        compiler_params=_params("parallel", "parallel"),
        name="ssm_prompt",
    )(us, wk, ws, wc, dec)


def _ssm_sample_body(us_ref, h0r_ref, h0i_ref, ws_ref, wc0_ref, dec_ref, y_ref, hr_ref, hi_ref):
    bu = _dot(us_ref[...].astype(BF16), ws_ref[...])
    a_r = dec_ref[0:1, :SG_STATE]
    a_i = dec_ref[0:1, SG_STATE:]
    dr, di = _cmul(a_r, a_i, h0r_ref[...], h0i_ref[...])
    h_re = bu[:, :SG_STATE] + dr
    h_im = bu[:, SG_STATE:] + di
    hr_ref[...] = h_re
    hi_ref[...] = h_im
    y_ref[...] = _dot(jnp.concatenate([h_re, h_im], axis=1).astype(BF16), wc0_ref[...])


def _ssm_sample(us, h0_re, h0_im, layer, mats):
    _, ws, _, wc0, dec = mats
    b = us.shape[0]
    st_spec = pl.BlockSpec((None, b, SG_STATE), lambda g: (g, 0, 0))
    return pl.pallas_call(
        _ssm_sample_body,
        grid=(N_SG,),
        in_specs=[pl.BlockSpec((b, LANES), lambda g: (0, g)), st_spec, st_spec,
                  pl.BlockSpec((None, None, LANES, 2 * SG_STATE), lambda g: (layer, g, T_CHUNK - 1, 0)),
                  pl.BlockSpec((None, None, 2 * SG_STATE, LANES), lambda g: (layer, g, 0, 0)),
                  pl.BlockSpec((None, None, N_DEC_ROWS, 2 * SG_STATE), lambda g: (layer, g, 0, 0))],
        out_specs=[pl.BlockSpec((b, LANES), lambda g: (0, g)), st_spec, st_spec],
        out_shape=[jax.ShapeDtypeStruct((b, SSM_WIDTH), F32),
                   jax.ShapeDtypeStruct((N_SG, b, SG_STATE), F32),
                   jax.ShapeDtypeStruct((N_SG, b, SG_STATE), F32)],
        compiler_params=_params("parallel"),
        name="ssm_sample",
    )(us, h0_re, h0_im, ws, wc0, dec)


def _gelu_tanh(x):
    return 0.5 * x * (1.0 + jnp.tanh(math.sqrt(2.0 / math.pi) * (x + 0.044715 * (x * x * x))))


def _merge_body(h_ref, o_ref, y_ref, us_ref, g_ref, wgate_ref, bgate_ref, d_ref, gluw_ref, glub_ref,
                wa_ref, ws_ref, wo_ref, out_ref):
    h = h_ref[...]
    u = _rms(h, g_ref[...]).astype(BF16)
    gates = jax.nn.sigmoid(_dot(u, wgate_ref[...]) + bgate_ref[...])
    y = _gelu_tanh(y_ref[...] + d_ref[...] * us_ref[...])
    ys = y * jax.nn.sigmoid(_dot(y.astype(BF16), gluw_ref[...]) + glub_ref[...])
    a = _dot(o_ref[...], wa_ref[...])
    s = _dot(ys.astype(BF16), ws_ref[...])
    mix = gates[:, :D_MODEL] * a + gates[:, D_MODEL:] * s
    out_ref[...] = h + _dot(mix.astype(BF16), wo_ref[...])


def _merge(h, o, y, us, layer, w, tm=512):
    m = h.shape[0]
    tm = min(tm, m)
    row = lambda n: pl.BlockSpec((tm, n), lambda i: (i, 0))
    return pl.pallas_call(
        _merge_body,
        grid=(m // tm,),
        in_specs=[row(D_MODEL), row(ATTN_WIDTH), row(SSM_WIDTH), row(SSM_WIDTH),
                  _const_spec((None, 1, D_MODEL), (layer, 0, 0)),
                  _const_spec((None, D_MODEL, 2 * D_MODEL), (layer, 0, 1)),
                  _const_spec((None, 1, 2 * D_MODEL), (layer, 0, 0)),
                  _const_spec((None, 1, SSM_WIDTH), (layer, 0, 0)),
                  _const_spec((None, SSM_WIDTH, SSM_WIDTH), (layer, 0, 0)),
                  _const_spec((None, 1, SSM_WIDTH), (layer, 0, 0)),
                  _const_spec((None, ATTN_WIDTH, D_MODEL), (layer, 0, 0)),
                  _const_spec((None, SSM_WIDTH, D_MODEL), (layer, 0, 0)),
                  _const_spec((None, D_MODEL, D_MODEL), (layer, 0, 0))],
        out_specs=row(D_MODEL),
        out_shape=jax.ShapeDtypeStruct((m, D_MODEL), F32),
        compiler_params=_params("parallel"),
        name="merge",
    )(h, o, y, us, w["norm_mix"], w["w_in"], w["b_gate"], w["ssm_d"], w["glu_w"], w["glu_b"],
      w["w_attn_out"], w["w_ssm_out"], w["w_out"])


def _rope_tables(pos):
    half = HEAD_DIM // 2
    inv = ROPE_THETA ** (-jnp.arange(half, dtype=F32) / half)
    ang = pos.astype(F32)[:, None] * inv[None, :]
    cos = jnp.cos(ang)
    sin = jnp.sin(ang)
    reps = ATTN_WIDTH // HEAD_DIM
    cos = jnp.tile(jnp.concatenate([cos, cos], axis=1), (1, reps))
    sin = jnp.tile(jnp.concatenate([-sin, sin], axis=1), (1, reps))
    return cos, sin, cos.T, sin.T


def kernel(x_prompt, x_sample, cache_k, cache_v, state_ssm_re, state_ssm_im, page_table, norm_ffn1, ffn1_w_gate, ffn1_w_up, ffn1_w_down, norm_mix, w_in, b_gate, lambda_q1, lambda_k1, lambda_q2, lambda_k2, attn_subln, w_attn_out, ssm_a_re, ssm_a_im, ssm_log_dt, ssm_b_re, ssm_b_im, ssm_c_re, ssm_c_im, ssm_d, glu_w, glu_b, w_ssm_out, w_out, norm_ffn2, ffn2_w_gate, ffn2_w_up, ffn2_w_down, final_norm):
    batch, seq, _ = x_prompt.shape
    dec_batch, dec_seq, _ = x_sample.shape
    depth = w_in.shape[0]
    n_pool = cache_k.shape[1]
    past_len = page_table.shape[1] * PAGE_SIZE
    assert dec_seq == 1

    vec = lambda x: x.reshape(depth, 1, -1)
    bf = lambda x: x.astype(BF16)
    w_in_b = bf(w_in)
    w_kt = bf(jnp.swapaxes(w_in[:, :, ATTN_WIDTH:2 * ATTN_WIDTH], 1, 2))
    mw = dict(norm_mix=vec(norm_mix), w_in=w_in_b, b_gate=vec(b_gate), ssm_d=vec(ssm_d), glu_w=bf(glu_w),
              glu_b=vec(glu_b), w_attn_out=bf(w_attn_out), w_ssm_out=bf(w_ssm_out), w_out=bf(w_out))
    f1 = (vec(norm_ffn1), bf(ffn1_w_gate), bf(ffn1_w_up), bf(ffn1_w_down))
    f2 = (vec(norm_ffn2), bf(ffn2_w_gate), bf(ffn2_w_up), bf(ffn2_w_down))
    lam_rows = jnp.stack([lambda_q1, lambda_k1, lambda_q2, lambda_k2], axis=1)
    subln = vec(attn_subln)
    final_g = final_norm.reshape(1, D_MODEL)

    tabs_p = _rope_tables(jnp.arange(seq, dtype=jnp.int32))
    tabs_s = _rope_tables(jnp.full((dec_batch,), past_len, jnp.int32))
    cache_kt = jnp.transpose(cache_k, (0, 1, 3, 4, 2)).reshape(depth, n_pool, ATTN_WIDTH, PAGE_SIZE)
    cache_v2 = cache_v.reshape(depth, n_pool, PAGE_SIZE * N_HEADS, 2 * HEAD_DIM)

    mats = _ssm_prep(ssm_a_re, ssm_a_im, ssm_log_dt, ssm_b_re, ssm_b_im, ssm_c_re, ssm_c_im)
    sg_state = lambda x: x.reshape(dec_batch, N_SG, SG_STATE).transpose(1, 0, 2)
    sg_unstate = lambda x: x.transpose(1, 0, 2).reshape(dec_batch, N_GROUPS, STATE_DIM)

    xp = x_prompt.reshape(batch * seq, D_MODEL)
    xs = x_sample.reshape(dec_batch, D_MODEL)
    outs = [[] for _ in range(8)]
    kt_all = jnp.zeros((depth, batch, ATTN_WIDTH, seq), F32)
    v4_all = jnp.zeros((depth, batch, seq * N_HEADS, 2 * HEAD_DIM), F32)
    for l in range(depth):
        lam_init = 0.8 - 0.6 * math.exp(-0.3 * l)
        last = l == depth - 1

        hs = _ffn(xs, l, *f1)
        q, kt, _, v4, _, us_s, k_new = _inproj(hs.reshape(1, dec_batch, D_MODEL), l, mw["norm_mix"], w_in_b, w_kt,
                                               tabs_s, k_rows=True)
        v_new4 = v4.reshape(dec_batch, N_HEADS, 2 * HEAD_DIM)
        dec = (page_table, lam_rows, subln, q.reshape(dec_batch, 1, ATTN_WIDTH),
               k_new.reshape(dec_batch, 1, ATTN_WIDTH), v_new4, cache_kt, cache_v2, lam_init)

        h, o_lo = _ffn_decode(xp, l, *f1, dec, 0)
        q, kt_all, ktb, v4_all, vb, us = _inproj(h.reshape(batch, seq, D_MODEL), l, mw["norm_mix"], w_in_b, w_kt,
                                                 tabs_p, stacked=(kt_all, v4_all))
        o = _attn_prompt(q, ktb, vb, l, lam_rows, subln, lam_init)
        y, st = _ssm_prompt(us, l, mats)
        h = _merge(h, o.reshape(batch * seq, ATTN_WIDTH), y.reshape(batch * seq, SSM_WIDTH),
                   us.reshape(batch * seq, SSM_WIDTH), l, mw)
        xp, o_hi = _ffn_decode(h, l, *f2, dec, o_lo.shape[0], final_g=final_g if last else None)
        assert o_lo.shape[0] + o_hi.shape[0] == dec_batch
        outs[2].append(st[:, :, 0, :].reshape(batch, N_GROUPS, STATE_DIM))
        outs[3].append(st[:, :, 1, :].reshape(batch, N_GROUPS, STATE_DIM))

        o = jnp.concatenate([o_lo, o_hi], axis=0)
        us = us_s.reshape(dec_batch, SSM_WIDTH)
        y, h_re, h_im = _ssm_sample(us, sg_state(state_ssm_re[l]), sg_state(state_ssm_im[l]), l, mats)
        h = _merge(hs, o.reshape(dec_batch, ATTN_WIDTH), y, us, l, mw)
        xs = _ffn(h, l, *f2, final_g=final_g if last else None)
        outs[4].append(kt.reshape(2 * N_HEADS, HEAD_DIM, dec_batch).transpose(2, 0, 1)[:, None])
        outs[5].append(v_new4[:, None])
        outs[6].append(sg_unstate(h_re))
        outs[7].append(sg_unstate(h_im))

    stk = [jnp.stack(o) for o in outs[2:]]
    k_prompt = kt_all.reshape(depth, batch, 2 * N_HEADS, HEAD_DIM, seq).transpose(0, 1, 4, 2, 3)
    v_prompt = v4_all.reshape(depth, batch, seq, N_HEADS, 2 * HEAD_DIM)
    return (xp.reshape(batch, seq, D_MODEL), xs.reshape(dec_batch, dec_seq, D_MODEL),
            k_prompt, v_prompt, stk[0], stk[1], stk[2], stk[3], stk[4], stk[5])
```

```python
import functools
import math

import jax
import jax.numpy as jnp
from jax import lax
from jax.experimental import pallas as pl
from jax.experimental.pallas import tpu as pltpu

F32 = jnp.float32
BF16 = jnp.bfloat16

D_MODEL = 1024
N_HEADS = 4
HEAD_DIM = 64
ATTN_WIDTH = 2 * N_HEADS * HEAD_DIM
SSM_WIDTH = 512
GROUP_SIZE = 16
N_GROUPS = SSM_WIDTH // GROUP_SIZE
STATE_DIM = 64
D_FF = 2816
ROPE_THETA = 10000.0
NORM_EPS = 1e-6
PAGE_SIZE = 128

LANES = 128
SG_GROUPS = LANES // GROUP_SIZE
N_SG = N_GROUPS // SG_GROUPS
SG_STATE = SG_GROUPS * STATE_DIM
T_CHUNK = 8
N_DOUBLING = 16
SCAN_GROUP = 16
N_DEC_ROWS = N_DOUBLING + SCAN_GROUP
FF_CHUNK = 256
VMEM_LIMIT = 56 * 2**20

_HI = lax.Precision.HIGHEST


def _dot(a, b):
    return jnp.dot(a, b, preferred_element_type=F32)


def _rms(x, g):
    return x * lax.rsqrt(jnp.mean(x * x, axis=-1, keepdims=True) + NORM_EPS) * g


def _tree(op, xs):
    xs = list(xs)
    while len(xs) > 1:
        xs = [op(xs[k], xs[k + 1]) if k + 1 < len(xs) else xs[k] for k in range(0, len(xs), 2)]
    return xs[0]


def _params(*sem):
    return pltpu.CompilerParams(dimension_semantics=sem, vmem_limit_bytes=VMEM_LIMIT)


def _const_spec(shape, index):
    return pl.BlockSpec(shape, lambda *_: index, pipeline_mode=pl.Buffered(1))


def _swiglu_chunks(u, wg_ref, wu_ref, wd_ref, acc, first, last):
    for c in range(first, last):
        cols = slice(c * FF_CHUNK, (c + 1) * FF_CHUNK)
        gate = _dot(u, wg_ref[:, cols])
        up = _dot(u, wu_ref[:, cols])
        act = (gate * jax.nn.sigmoid(gate) * up).astype(BF16)
        acc = acc + _dot(act, wd_ref[cols, :])
    return acc


def _ffn_body(*refs, final):
    if final:
        x_ref, g_ref, wg_ref, wu_ref, wd_ref, fg_ref, o_ref = refs
    else:
        x_ref, g_ref, wg_ref, wu_ref, wd_ref, o_ref = refs
    x = x_ref[...]
    u = _rms(x, g_ref[...]).astype(BF16)
    acc = _swiglu_chunks(u, wg_ref, wu_ref, wd_ref, jnp.zeros(x.shape, F32), 0, D_FF // FF_CHUNK)
    y = x + 0.5 * acc
    if final:
        y = _rms(y, fg_ref[...])
    o_ref[...] = y


def _ffn(x, layer, g, wg, wu, wd, final_g=None, tm=512):
    m = x.shape[0]
    tm = min(tm, m)
    in_specs = [
        pl.BlockSpec((tm, D_MODEL), lambda i: (i, 0)),
        _const_spec((None, 1, D_MODEL), (layer, 0, 0)),
        _const_spec((None, D_MODEL, D_FF), (layer, 0, 0)),
        _const_spec((None, D_MODEL, D_FF), (layer, 0, 0)),
        _const_spec((None, D_FF, D_MODEL), (layer, 0, 0)),
    ]
    args = [x, g, wg, wu, wd]
    if final_g is not None:
        in_specs.append(_const_spec((1, D_MODEL), (0, 0)))
        args.append(final_g)
    return pl.pallas_call(
        functools.partial(_ffn_body, final=final_g is not None),
        grid=(m // tm,),
        in_specs=in_specs,
        out_specs=pl.BlockSpec((tm, D_MODEL), lambda i: (i, 0)),
        out_shape=jax.ShapeDtypeStruct((m, D_MODEL), F32),
        compiler_params=_params("parallel"),
        name="ffn",
    )(*args)


def _swap_halves_lanes(x):
    n = x.shape[-1]
    half = HEAD_DIM // 2
    lane = lax.broadcasted_iota(jnp.int32, x.shape, x.ndim - 1)
    return jnp.where(lane % HEAD_DIM < half, pltpu.roll(x, n - half, x.ndim - 1),
                     pltpu.roll(x, half, x.ndim - 1))


def _swap_halves_rows(x):
    half = HEAD_DIM // 2
    parts = []
    for r in range(0, x.shape[0], HEAD_DIM):
        parts += [x[r + half:r + HEAD_DIM], x[r:r + half]]
    return jnp.concatenate(parts, axis=0)


def _inproj_body(*refs, k_rows, n_aliased):
    n_in = 10 + (1 if k_rows else 0) + n_aliased
    x_ref, g_ref, wq_ref, wkt_ref, wv_ref, wus_ref, cos_ref, sin_ref, cost_ref, sint_ref = refs[:10]
    q_ref, kt_ref, ktb_ref, v4_ref, vb_ref, us_ref = refs[n_in:n_in + 6]
    tm = x_ref.shape[0]
    u = _rms(x_ref[...], g_ref[...]).astype(BF16)
    cos = cos_ref[...]
    sin = sin_ref[...]
    q = _dot(u, wq_ref[...])
    q = q * cos + _swap_halves_lanes(q) * sin
    q_ref[...] = (q * (HEAD_DIM ** -0.5)).astype(BF16)
    kt = lax.dot_general(wkt_ref[...], u, (((1,), (1,)), ((), ())), preferred_element_type=F32)
    kt = kt * cost_ref[...] + _swap_halves_rows(kt) * sint_ref[...]
    kt_ref[...] = kt
    ktb_ref[...] = kt.astype(BF16)
    v = _dot(u, wv_ref[...])
    vb_ref[...] = v.astype(BF16)
    for h in range(N_HEADS):
        v4_ref[pl.ds(h, tm, stride=N_HEADS), :] = v[:, h * LANES:(h + 1) * LANES]
    us_ref[...] = _dot(u, wus_ref[...])
    if k_rows:
        wk_ref, k_ref = refs[10], refs[n_in + 6]
        k = _dot(u, wk_ref[...])
        k_ref[...] = k * cos + _swap_halves_lanes(k) * sin


def _inproj(x, layer, g, w_in, w_kt, tabs, k_rows=False, stacked=None, tm=512):
    b, l, _ = x.shape
    tm = min(tm, l)
    cos, sin, cost, sint = tabs
    aw = ATTN_WIDTH
    wcol = lambda c: _const_spec((None, D_MODEL, aw), (layer, 0, c))
    in_specs = [
        pl.BlockSpec((None, tm, D_MODEL), lambda j, i: (i, j, 0)),
        _const_spec((None, 1, D_MODEL), (layer, 0, 0)),
        wcol(0),
        _const_spec((None, aw, D_MODEL), (layer, 0, 0)),
        wcol(2),
        wcol(3),
        pl.BlockSpec((tm, aw), lambda j, i: (j, 0)),
        pl.BlockSpec((tm, aw), lambda j, i: (j, 0)),
        pl.BlockSpec((aw, tm), lambda j, i: (0, j)),
        pl.BlockSpec((aw, tm), lambda j, i: (0, j)),
    ]
    args = [x, g, w_in, w_kt, w_in, w_in, cos, sin, cost, sint]
    row_spec = pl.BlockSpec((None, tm, aw), lambda j, i: (i, j, 0))
    col_spec = pl.BlockSpec((None, aw, tm), lambda j, i: (i, 0, j))
    out_specs = [row_spec, col_spec, col_spec,
                 pl.BlockSpec((None, tm * N_HEADS, LANES), lambda j, i: (i, j, 0)),
                 row_spec, row_spec]
    out_shape = [jax.ShapeDtypeStruct((b, l, aw), BF16),
                 jax.ShapeDtypeStruct((b, aw, l), F32),
                 jax.ShapeDtypeStruct((b, aw, l), BF16),
                 jax.ShapeDtypeStruct((b, l * N_HEADS, LANES), F32),
                 jax.ShapeDtypeStruct((b, l, aw), BF16),
                 jax.ShapeDtypeStruct((b, l, SSM_WIDTH), F32)]
    if k_rows:
        in_specs.append(wcol(1))
        args.append(w_in)
        out_specs.append(row_spec)
        out_shape.append(jax.ShapeDtypeStruct((b, l, aw), F32))
    aliases = {}
    if stacked is not None:
        for buf, out_idx in zip(stacked, (1, 3)):
            aliases[len(args)] = out_idx
            in_specs.append(pl.BlockSpec(memory_space=pl.ANY))
            args.append(buf)
            out_shape[out_idx] = jax.ShapeDtypeStruct(buf.shape, buf.dtype)
        out_specs[1] = pl.BlockSpec((None, None, aw, tm), lambda j, i: (layer, i, 0, j))
        out_specs[3] = pl.BlockSpec((None, None, tm * N_HEADS, LANES), lambda j, i: (layer, i, j, 0))
    return pl.pallas_call(
        functools.partial(_inproj_body, k_rows=k_rows, n_aliased=len(aliases)),
        grid=(l // tm, b),
        in_specs=in_specs,
        out_specs=out_specs,
        out_shape=out_shape,
        input_output_aliases=aliases,
        compiler_params=_params("parallel", "parallel"),
        name="in_proj",
    )(*args)


def _lambda(lam_ref, lam_init):
    a = jnp.sum(lam_ref[0:1, :] * lam_ref[1:2, :], axis=-1, keepdims=True)
    b = jnp.sum(lam_ref[2:3, :] * lam_ref[3:4, :], axis=-1, keepdims=True)
    return jnp.exp(a) - jnp.exp(b) + lam_init


def _head_norm(o, subln, lam_init):
    return _rms(o, subln) * (1.0 - lam_init)


def _attn_body(lam_ref, subln_ref, q_ref, kt_ref, v_ref, o_ref, s_scr, *, lam_init, tq, tk):
    n_tiles = q_ref.shape[0] // tq
    lam = _lambda(lam_ref, lam_init)
    subln = subln_ref[...]
    lane = lax.broadcasted_iota(jnp.int32, (tq, LANES), 1)
    row = lax.broadcasted_iota(jnp.int32, (tq, tk), 0)
    col = lax.broadcasted_iota(jnp.int32, (tq, tk), 1)

    def one_map(i, which, qm):
        n_kv = -(-(i + 1) * tq // tk)
        m_tile = jnp.full((tq, LANES), -jnp.inf, F32)
        for j in range(n_kv):
            s = _dot(qm, kt_ref[:, j * tk:(j + 1) * tk])
            if j == n_kv - 1:
                s = jnp.where(col + j * tk <= row + i * tq, s, -jnp.inf)
            s_scr[which, :, j * tk:(j + 1) * tk] = s
            for c in range(tk // LANES):
                m_tile = jnp.maximum(m_tile, s[:, c * LANES:(c + 1) * LANES])
        m_b = jnp.broadcast_to(jnp.max(m_tile, axis=-1, keepdims=True), (tq, LANES))
        l_tile = jnp.zeros((tq, LANES), F32)
        acc = jnp.zeros((tq, LANES), F32)
        for j in range(n_kv):
            parts = []
            for c in range(tk // LANES):
                p = jnp.exp(s_scr[which, :, j * tk + c * LANES:j * tk + (c + 1) * LANES] - m_b)
                l_tile = l_tile + p
                parts.append(p.astype(BF16))
            acc = acc + _dot(jnp.concatenate(parts, axis=1), v_ref[j * tk:(j + 1) * tk, :])
        return acc / jnp.sum(l_tile, axis=-1, keepdims=True)

    for i in range(n_tiles):
        rows = slice(i * tq, (i + 1) * tq)
        q = q_ref[rows, :]
        zero = jnp.zeros_like(q)
        o1 = one_map(i, 0, jnp.where(lane < HEAD_DIM, q, zero))
        o2 = one_map(i, 1, jnp.where(lane >= HEAD_DIM, q, zero))
        o_ref[rows, :] = _head_norm(o1 - lam * o2, subln, lam_init).astype(o_ref.dtype)


def _attn_prompt(q, ktb, vb, layer, lam_rows, subln, lam_init, tq=256, tk=256):
    b, l, _ = q.shape
    assert tk % tq == 0 and l % tk == 0
    rows = pl.BlockSpec((None, l, LANES), lambda i, h: (i, 0, h))
    return pl.pallas_call(
        functools.partial(_attn_body, lam_init=lam_init, tq=tq, tk=tk),
        grid=(b, N_HEADS),
        in_specs=[
            _const_spec((None, 4, HEAD_DIM), (layer, 0, 0)),
            _const_spec((None, 1, LANES), (layer, 0, 0)),
            rows,
            pl.BlockSpec((None, LANES, l), lambda i, h: (i, h, 0)),
            rows,
        ],
        out_specs=rows,
        out_shape=jax.ShapeDtypeStruct((b, l, ATTN_WIDTH), BF16),
        scratch_shapes=[pltpu.VMEM((2, tq, l), F32)],
        compiler_params=_params("parallel", "parallel"),
        name="attn_prompt",
    )(lam_rows, subln, q, ktb, vb)


def _rows_to_row_sums(tiles):
    row = lax.broadcasted_iota(jnp.int32, tiles[0].shape, 0)
    step = 4
    while len(tiles) > 1:
        half = len(tiles) // 2
        low = (row & step) == 0
        nxt = []
        for j in range(half):
            a, b = tiles[j], tiles[j + half]
            keep = jnp.where(low, a, b)
            other = jnp.where(low, b, a)
            partner = jnp.where(low, pltpu.roll(other, 8 - step, 0), pltpu.roll(other, step, 0))
            nxt.append(keep + partner)
        tiles = nxt
        step //= 2
    return tiles[0]


def _decode_one(q, k_new, v_new, k_refs, v_refs, lam, subln, lam_init):
    n_sub = 2 * N_HEADS
    sub_of_row = [2 * r if r < N_HEADS else 2 * (r - N_HEADS) + 1 for r in range(n_sub)]
    qf = q.astype(F32)
    row = lax.broadcasted_iota(jnp.int32, (n_sub, ATTN_WIDTH), 0)
    lane = lax.broadcasted_iota(jnp.int32, (n_sub, ATTN_WIDTH), 1)
    sub = jnp.where(row < N_HEADS, 2 * row, 2 * (row - N_HEADS) + 1)
    q_blk = jnp.where(lane // HEAD_DIM == sub, qf, 0.0)
    s_new = jnp.sum(q_blk * k_new, axis=-1, keepdims=True)
    q_col = jnp.transpose(jnp.broadcast_to(qf, (LANES, ATTN_WIDTH)))
    scores = []
    for k_ref in k_refs:
        prod = k_ref[...] * q_col
        tiles = [jnp.sum(prod[s * HEAD_DIM:(s + 1) * HEAD_DIM].reshape(HEAD_DIM // 8, 8, LANES), axis=0)
                 for s in sub_of_row]
        scores.append(_rows_to_row_sums(tiles))
    m = jnp.maximum(s_new, jnp.max(_tree(jnp.maximum, scores), axis=-1, keepdims=True))
    p_new = jnp.exp(s_new - m)
    probs = [jnp.exp(s - m) for s in scores]
    inv = 1.0 / (p_new + jnp.sum(_tree(jnp.add, probs), axis=-1, keepdims=True))
    r8 = lax.broadcasted_iota(jnp.int32, (n_sub, 1), 0)
    coef = jnp.where(r8 < N_HEADS, inv, -lam * inv)

    def fold(x):
        y = x * coef
        return y + pltpu.roll(y, N_HEADS, 0)

    tiles_per_page = PAGE_SIZE * N_HEADS // 8
    j3 = lax.broadcasted_iota(jnp.int32, (tiles_per_page, 8, LANES), 0)
    r3 = lax.broadcasted_iota(jnp.int32, (tiles_per_page, 8, LANES), 1)
    l3 = lax.broadcasted_iota(jnp.int32, (tiles_per_page, 8, LANES), 2)
    pick = l3 == 2 * j3 + r3 // N_HEADS
    terms = []
    for p, v_ref in zip(probs, v_refs):
        w = jnp.sum(jnp.where(pick, fold(p)[None], 0.0), axis=-1, keepdims=True)
        terms.append(jnp.sum(w * v_ref[...].reshape(tiles_per_page, 8, LANES), axis=0))
    acc = _tree(jnp.add, terms)
    w_new = fold(jnp.broadcast_to(p_new, (n_sub, LANES)))
    o = acc[:N_HEADS] + acc[N_HEADS:] + w_new[:N_HEADS] * v_new
    return _head_norm(o, subln, lam_init)


def _page_copies(pt_ref, ckt_hbm, cv_hbm, kbuf, vbuf, sems, layer, sample, slot):
    copies = []
    for p in range(kbuf.shape[1]):
        page = pt_ref[sample, p]
        copies.append(pltpu.make_async_copy(ckt_hbm.at[layer, page], kbuf.at[slot, p], sems.at[0, slot, p]))
        copies.append(pltpu.make_async_copy(cv_hbm.at[layer, page], vbuf.at[slot, p], sems.at[1, slot, p]))
    return copies


DEC_PER_STEP = 2


def _ffn_decode_body(*refs, final, lam_init, layer, sample_base):
    n_in = 14 if final else 13
    pt_ref, x_ref, g_ref, wg_ref, wu_ref, wd_ref = refs[:6]
    fg_ref = refs[6] if final else None
    lam_ref, subln_ref, q_ref, knew_ref, vnew_ref, ckt_hbm, cv_hbm = refs[n_in - 7:n_in]
    y_ref, od_ref, kbuf, vbuf, sems = refs[n_in:]
    n_pages = kbuf.shape[1]
    i = pl.program_id(0)
    more = i + 1 < pl.num_programs(0)
    first = sample_base + DEC_PER_STEP * i
    copies = functools.partial(_page_copies, pt_ref, ckt_hbm, cv_hbm, kbuf, vbuf, sems, layer)
    lam = _lambda(lam_ref, lam_init)
    subln = subln_ref[...]

    def decode(s):
        k_refs = [kbuf.at[s, p] for p in range(n_pages)]
        v_refs = [vbuf.at[s, p] for p in range(n_pages)]
        o = _decode_one(q_ref[s], knew_ref[s], vnew_ref[s], k_refs, v_refs, lam, subln, lam_init)
        od_ref[s] = o.astype(od_ref.dtype)

    @pl.when(i == 0)
    def _():
        for s in range(DEC_PER_STEP):
            for c in copies(first + s, s):
                c.start()

    x = x_ref[...]
    u = _rms(x, g_ref[...]).astype(BF16)
    acc = jnp.zeros(x.shape, F32)
    n_chunks = D_FF // FF_CHUNK
    bounds = [(s * n_chunks) // DEC_PER_STEP for s in range(DEC_PER_STEP + 1)]
    for s in range(DEC_PER_STEP):
        for c in copies(first + s, s):
            c.wait()
        acc = _swiglu_chunks(u, wg_ref, wu_ref, wd_ref, acc, bounds[s], bounds[s + 1])
        decode(s)

        @pl.when(more)
        def _():
            for c in copies(first + DEC_PER_STEP + s, s):
                c.start()

    y = x + 0.5 * acc
    if final:
        y = _rms(y, fg_ref[...])
    y_ref[...] = y


def _ffn_decode(x, layer, g, wg, wu, wd, dec, sample_base, final_g=None, tm=512):
    page_table, lam_rows, subln, q, k_new, v_new4, cache_kt, cache_v, lam_init = dec
    m = x.shape[0]
    steps = m // tm
    n_dec = DEC_PER_STEP * steps
    assert sample_base % DEC_PER_STEP == 0 and sample_base + n_dec <= q.shape[0]
    n_pages = page_table.shape[1]
    page_shape = (DEC_PER_STEP, n_pages, ATTN_WIDTH, LANES)
    const = lambda shape, index: pl.BlockSpec(shape, lambda i, pt: index, pipeline_mode=pl.Buffered(1))
    per_sample = lambda *shape: pl.BlockSpec((DEC_PER_STEP,) + shape,
                                             lambda i, pt: (sample_base // DEC_PER_STEP + i, 0, 0))
    in_specs = [
        pl.BlockSpec((tm, D_MODEL), lambda i, pt: (i, 0)),
        const((None, 1, D_MODEL), (layer, 0, 0)),
        const((None, D_MODEL, D_FF), (layer, 0, 0)),
        const((None, D_MODEL, D_FF), (layer, 0, 0)),
        const((None, D_FF, D_MODEL), (layer, 0, 0)),
    ]
    args = [x, g, wg, wu, wd]
    if final_g is not None:
        in_specs.append(const((1, D_MODEL), (0, 0)))
        args.append(final_g)
    in_specs += [
        const((None, 4, HEAD_DIM), (layer, 0, 0)),
        const((None, 1, LANES), (layer, 0, 0)),
        per_sample(1, ATTN_WIDTH),
        per_sample(1, ATTN_WIDTH),
        per_sample(N_HEADS, LANES),
        pl.BlockSpec(memory_space=pl.ANY),
        pl.BlockSpec(memory_space=pl.ANY),
    ]
    args += [lam_rows, subln, q, k_new, v_new4, cache_kt, cache_v]
    grid_spec = pltpu.PrefetchScalarGridSpec(
        num_scalar_prefetch=1,
        grid=(steps,),
        in_specs=in_specs,
        out_specs=[pl.BlockSpec((tm, D_MODEL), lambda i, pt: (i, 0)),
                   pl.BlockSpec((DEC_PER_STEP, N_HEADS, LANES), lambda i, pt: (i, 0, 0))],
        scratch_shapes=[pltpu.VMEM(page_shape, F32), pltpu.VMEM(page_shape, F32),
                        pltpu.SemaphoreType.DMA((2, DEC_PER_STEP, n_pages))],
    )
    return pl.pallas_call(
        functools.partial(_ffn_decode_body, final=final_g is not None, lam_init=lam_init, layer=layer,
                          sample_base=sample_base),
        grid_spec=grid_spec,
        out_shape=[jax.ShapeDtypeStruct((m, D_MODEL), F32),
                   jax.ShapeDtypeStruct((n_dec, N_HEADS, LANES), BF16)],
        compiler_params=_params("arbitrary"),
        name="ffn_decode",
    )(page_table, *args)


def _cmul(ar, ai, br, bi):
    return ar * br - ai * bi, ar * bi + ai * br


def _ssm_prep_body(are_ref, aim_ref, ldt_ref, btr_ref, bti_ref, ctr_ref, cti_ref,
                   wk_ref, ws_ref, wc_ref, wc0_ref, dec_ref):
    t = T_CHUNK
    a_re = are_ref[...]
    a_im = aim_ref[...]
    dt = jnp.exp(ldt_ref[...])
    mag = jnp.exp(a_re * dt)
    p_re = mag * jnp.cos(a_im * dt)
    p_im = mag * jnp.sin(a_im * dt)
    den = a_re * a_re + a_im * a_im
    nr = p_re - 1.0
    coef_re = (nr * a_re + p_im * a_im) / den
    coef_im = (p_im * a_re - nr * a_im) / den

    r = lax.broadcasted_iota(jnp.int32, (LANES, SG_STATE), 0)
    c = lax.broadcasted_iota(jnp.int32, (LANES, SG_STATE), 1)
    bt_mask = r // GROUP_SIZE == c // STATE_DIM
    r = lax.broadcasted_iota(jnp.int32, (SG_STATE, LANES), 0)
    c = lax.broadcasted_iota(jnp.int32, (SG_STATE, LANES), 1)
    ct_mask = r // STATE_DIM == c // GROUP_SIZE
    btr = jnp.where(bt_mask, btr_ref[...], 0.0)
    bti = jnp.where(bt_mask, bti_ref[...], 0.0)
    ctr = jnp.where(ct_mask, ctr_ref[...], 0.0)
    cti = jnp.where(ct_mask, cti_ref[...], 0.0)
    bbr, bbi = _cmul(coef_re, coef_im, btr, bti)

    pow_re = [jnp.ones_like(p_re)]
    pow_im = [jnp.zeros_like(p_im)]
    col = lambda x: jnp.transpose(jnp.broadcast_to(x, (LANES, SG_STATE)))
    p_re_c, p_im_c = col(p_re), col(p_im)
    cpow_re = [jnp.ones_like(p_re_c)]
    cpow_im = [jnp.zeros_like(p_im_c)]
    for _ in range(t):
        nr_, ni_ = _cmul(pow_re[-1], pow_im[-1], p_re, p_im)
        pow_re.append(nr_)
        pow_im.append(ni_)
        nr_, ni_ = _cmul(cpow_re[-1], cpow_im[-1], p_re_c, p_im_c)
        cpow_re.append(nr_)
        cpow_im.append(ni_)

    ct_cat = jnp.concatenate([ctr, cti], axis=0)
    wk_ref[...] = jnp.zeros(wk_ref.shape, wk_ref.dtype)
    for j in range(t):
        er, ei = _cmul(bbr, bbi, pow_re[j], pow_im[j])
        kj = jnp.dot(jnp.concatenate([er, -ei], axis=1), ct_cat, precision=_HI,
                     preferred_element_type=F32).astype(BF16)
        for t_in in range(t - j):
            wk_ref[t_in * LANES:(t_in + 1) * LANES, (t_in + j) * LANES:(t_in + j + 1) * LANES] = kj
        rows = slice((t - 1 - j) * LANES, (t - j) * LANES)
        ws_ref[rows, :SG_STATE] = er.astype(BF16)
        ws_ref[rows, SG_STATE:] = ei.astype(BF16)
    for j in range(1, t + 1):
        cols = slice((j - 1) * LANES, j * LANES)
        wc_ref[:SG_STATE, cols] = (ctr * cpow_re[j] - cti * cpow_im[j]).astype(BF16)
        wc_ref[SG_STATE:, cols] = (-(ctr * cpow_im[j] + cti * cpow_re[j])).astype(BF16)
    wc0_ref[:SG_STATE, :] = ctr.astype(BF16)
    wc0_ref[SG_STATE:, :] = (-cti).astype(BF16)

    dec_rows = [jnp.concatenate([p_re, p_im], axis=1)]
    dr, di = pow_re[t], pow_im[t]
    for _ in range(N_DOUBLING - 1):
        dec_rows.append(jnp.concatenate([dr, di], axis=1))
        dr, di = _cmul(dr, di, dr, di)
    lr, li = pow_re[t], pow_im[t]
    for _ in range(SCAN_GROUP):
        dec_rows.append(jnp.concatenate([lr, li], axis=1))
        lr, li = _cmul(lr, li, pow_re[t], pow_im[t])
    dec_ref[...] = jnp.concatenate(dec_rows, axis=0)


def _ssm_prep(a_re, a_im, log_dt, b_re, b_im, c_re, c_im):
    depth = a_re.shape[0]
    rowed = lambda x: x.reshape(depth, N_SG, 1, SG_STATE)
    ldt = jnp.broadcast_to(log_dt[:, :, None], (depth, N_GROUPS, STATE_DIM))

    def bt(x):
        y = x.reshape(depth, N_SG, SG_GROUPS, STATE_DIM, GROUP_SIZE).transpose(0, 1, 4, 2, 3)
        y = y.reshape(depth, N_SG, 1, GROUP_SIZE, SG_STATE)
        return jnp.broadcast_to(y, (depth, N_SG, SG_GROUPS, GROUP_SIZE, SG_STATE)).reshape(
            depth, N_SG, LANES, SG_STATE)

    def ct(x):
        y = x.reshape(depth, N_SG, SG_GROUPS, GROUP_SIZE, STATE_DIM).transpose(0, 1, 4, 2, 3)
        y = y.reshape(depth, N_SG, 1, STATE_DIM, LANES)
        return jnp.broadcast_to(y, (depth, N_SG, SG_GROUPS, STATE_DIM, LANES)).reshape(
            depth, N_SG, SG_STATE, LANES)

    tk = T_CHUNK * LANES
    blk = lambda *shape: pl.BlockSpec((None, None) + shape, lambda l, g: (l, g, 0, 0))
    return pl.pallas_call(
        _ssm_prep_body,
        grid=(depth, N_SG),
        in_specs=[blk(1, SG_STATE)] * 3 + [blk(LANES, SG_STATE)] * 2 + [blk(SG_STATE, LANES)] * 2,
        out_specs=[blk(tk, tk), blk(tk, 2 * SG_STATE), blk(2 * SG_STATE, tk),
                   blk(2 * SG_STATE, LANES), blk(N_DEC_ROWS, 2 * SG_STATE)],
        out_shape=[jax.ShapeDtypeStruct((depth, N_SG, tk, tk), BF16),
                   jax.ShapeDtypeStruct((depth, N_SG, tk, 2 * SG_STATE), BF16),
                   jax.ShapeDtypeStruct((depth, N_SG, 2 * SG_STATE, tk), BF16),
                   jax.ShapeDtypeStruct((depth, N_SG, 2 * SG_STATE, LANES), BF16),
                   jax.ShapeDtypeStruct((depth, N_SG, N_DEC_ROWS, 2 * SG_STATE), F32)],
        compiler_params=_params("parallel", "parallel"),
        name="ssm_prep",
    )(rowed(a_re), rowed(a_im), rowed(ldt), bt(b_re), bt(b_im), ct(c_re), ct(c_im))


def _ssm_prompt_body(us_ref, wk_ref, ws_ref, wc_ref, dec_ref, y_ref, st_ref, s_scr, c_scr):
    t = T_CHUNK
    n_chunks = us_ref.shape[0] // t
    x = jnp.concatenate([us_ref[pl.ds(i, n_chunks, stride=t), :] for i in range(t)], axis=1).astype(BF16)
    h_loc = _dot(x, ws_ref[...])
    s_re = h_loc[:, :SG_STATE]
    s_im = h_loc[:, SG_STATE:]
    row = lax.broadcasted_iota(jnp.int32, s_re.shape, 0)
    shifted = lambda v, d: jnp.where(row >= d, pltpu.roll(v, d, 0), 0.0)
    grp = SCAN_GROUP
    n_groups = n_chunks // grp

    def doubling(v_re, v_im, first_row, length, keep):
        d, i = 1, first_row
        while d < length:
            a_r = dec_ref[i:i + 1, :SG_STATE]
            a_i = dec_ref[i:i + 1, SG_STATE:]
            add_re, add_im = _cmul(a_r, a_i, jnp.where(keep(d), pltpu.roll(v_re, d, 0), 0.0),
                                   jnp.where(keep(d), pltpu.roll(v_im, d, 0), 0.0))
            v_re, v_im = v_re + add_re, v_im + add_im
            d, i = 2 * d, i + 1
        return v_re, v_im

    s_re, s_im = doubling(s_re, s_im, 1, grp, lambda d: row % grp >= d)
    def group_totals(k, v):
        n_col = SG_STATE // LANES
        for c in range(n_col):
            s_scr[k, c] = v[:, c * LANES:(c + 1) * LANES]
        return jnp.concatenate([s_scr[k, c, pl.ds(grp - 1, n_groups, stride=grp), :] for c in range(n_col)], axis=1)

    g_re = group_totals(0, s_re)
    g_im = group_totals(1, s_im)
    grow = lax.broadcasted_iota(jnp.int32, g_re.shape, 0)
    g_re, g_im = doubling(g_re, g_im, 1 + (grp.bit_length() - 1), n_groups, lambda d: grow >= d)
    c_scr[0] = jnp.where(grow >= 1, pltpu.roll(g_re, 1, 0), 0.0)
    c_scr[1] = jnp.where(grow >= 1, pltpu.roll(g_im, 1, 0), 0.0)
    spread = lambda k: jnp.concatenate(
        [jnp.broadcast_to(c_scr[k, g:g + 1, :], (grp, SG_STATE)) for g in range(n_groups)], axis=0)
    lin = dec_ref[N_DOUBLING:N_DOUBLING + grp, :]
    lin_re = jnp.concatenate([lin[:, :SG_STATE]] * n_groups, axis=0)
    lin_im = jnp.concatenate([lin[:, SG_STATE:]] * n_groups, axis=0)
    add_re, add_im = _cmul(lin_re, lin_im, spread(0), spread(1))
    s_re = s_re + add_re
    s_im = s_im + add_im
    st_ref[0:1, :] = s_re[n_chunks - 1:n_chunks, :]
    st_ref[1:2, :] = s_im[n_chunks - 1:n_chunks, :]
    h_in = jnp.concatenate([shifted(s_re, 1), shifted(s_im, 1)], axis=1).astype(BF16)
    y = _dot(x, wk_ref[...]) + _dot(h_in, wc_ref[...])
    for i in range(t):
        y_ref[pl.ds(i, n_chunks, stride=t), :] = y[:, i * LANES:(i + 1) * LANES]


def _ssm_prompt(us, layer, mats):
    wk, ws, wc, _, dec = mats
    b, l, _ = us.shape
    n_chunks = l // T_CHUNK
    assert n_chunks % SCAN_GROUP == 0 and n_chunks // SCAN_GROUP <= 2 ** (N_DOUBLING - 1) // SCAN_GROUP
    tk = T_CHUNK * LANES
    mat = lambda r, c: pl.BlockSpec((None, None, r, c), lambda g, i: (layer, g, 0, 0))
    return pl.pallas_call(
        _ssm_prompt_body,
        grid=(N_SG, b),
        in_specs=[pl.BlockSpec((None, l, LANES), lambda g, i: (i, 0, g)),
                  mat(tk, tk), mat(tk, 2 * SG_STATE), mat(2 * SG_STATE, tk),
                  mat(N_DEC_ROWS, 2 * SG_STATE)],
        out_specs=[pl.BlockSpec((None, l, LANES), lambda g, i: (i, 0, g)),
                   pl.BlockSpec((None, None, 2, SG_STATE), lambda g, i: (i, g, 0, 0))],
        out_shape=[jax.ShapeDtypeStruct((b, l, SSM_WIDTH), F32),
                   jax.ShapeDtypeStruct((b, N_SG, 2, SG_STATE), F32)],
        scratch_shapes=[pltpu.VMEM((2, SG_STATE // LANES, n_chunks, LANES), F32),
                        pltpu.VMEM((2, n_chunks // SCAN_GROUP, SG_STATE), F32)],
        compiler_params=_params("parallel", "parallel"),
        name="ssm_prompt",
    )(us, wk, ws, wc, dec)


def _ssm_sample_body(us_ref, h0r_ref, h0i_ref, ws_ref, wc0_ref, dec_ref, y_ref, hr_ref, hi_ref):
    bu = _dot(us_ref[...].astype(BF16), ws_ref[...])
    a_r = dec_ref[0:1, :SG_STATE]
    a_i = dec_ref[0:1, SG_STATE:]
    dr, di = _cmul(a_r, a_i, h0r_ref[...], h0i_ref[...])
    h_re = bu[:, :SG_STATE] + dr
    h_im = bu[:, SG_STATE:] + di
    hr_ref[...] = h_re
    hi_ref[...] = h_im
    y_ref[...] = _dot(jnp.concatenate([h_re, h_im], axis=1).astype(BF16), wc0_ref[...])


def _ssm_sample(us, h0_re, h0_im, layer, mats):
    _, ws, _, wc0, dec = mats
    b = us.shape[0]
    st_spec = pl.BlockSpec((None, b, SG_STATE), lambda g: (g, 0, 0))
    return pl.pallas_call(
        _ssm_sample_body,
        grid=(N_SG,),
        in_specs=[pl.BlockSpec((b, LANES), lambda g: (0, g)), st_spec, st_spec,
                  pl.BlockSpec((None, None, LANES, 2 * SG_STATE), lambda g: (layer, g, T_CHUNK - 1, 0)),
                  pl.BlockSpec((None, None, 2 * SG_STATE, LANES), lambda g: (layer, g, 0, 0)),
                  pl.BlockSpec((None, None, N_DEC_ROWS, 2 * SG_STATE), lambda g: (layer, g, 0, 0))],
        out_specs=[pl.BlockSpec((b, LANES), lambda g: (0, g)), st_spec, st_spec],
        out_shape=[jax.ShapeDtypeStruct((b, SSM_WIDTH), F32),
                   jax.ShapeDtypeStruct((N_SG, b, SG_STATE), F32),
                   jax.ShapeDtypeStruct((N_SG, b, SG_STATE), F32)],
        compiler_params=_params("parallel"),
        name="ssm_sample",
    )(us, h0_re, h0_im, ws, wc0, dec)


def _gelu_tanh(x):
    return 0.5 * x * (1.0 + jnp.tanh(math.sqrt(2.0 / math.pi) * (x + 0.044715 * (x * x * x))))


def _merge_body(h_ref, o_ref, y_ref, us_ref, g_ref, wgate_ref, bgate_ref, d_ref, gluw_ref, glub_ref,
                wa_ref, ws_ref, wo_ref, out_ref):
    h = h_ref[...]
    u = _rms(h, g_ref[...]).astype(BF16)
    gates = jax.nn.sigmoid(_dot(u, wgate_ref[...]) + bgate_ref[...])
    y = _gelu_tanh(y_ref[...] + d_ref[...] * us_ref[...])
    ys = y * jax.nn.sigmoid(_dot(y.astype(BF16), gluw_ref[...]) + glub_ref[...])
    a = _dot(o_ref[...], wa_ref[...])
    s = _dot(ys.astype(BF16), ws_ref[...])
    mix = gates[:, :D_MODEL] * a + gates[:, D_MODEL:] * s
    out_ref[...] = h + _dot(mix.astype(BF16), wo_ref[...])


def _merge(h, o, y, us, layer, w, tm=512):
    m = h.shape[0]
    tm = min(tm, m)
    row = lambda n: pl.BlockSpec((tm, n), lambda i: (i, 0))
    return pl.pallas_call(
        _merge_body,
        grid=(m // tm,),
        in_specs=[row(D_MODEL), row(ATTN_WIDTH), row(SSM_WIDTH), row(SSM_WIDTH),
                  _const_spec((None, 1, D_MODEL), (layer, 0, 0)),
                  _const_spec((None, D_MODEL, 2 * D_MODEL), (layer, 0, 1)),
                  _const_spec((None, 1, 2 * D_MODEL), (layer, 0, 0)),
                  _const_spec((None, 1, SSM_WIDTH), (layer, 0, 0)),
                  _const_spec((None, SSM_WIDTH, SSM_WIDTH), (layer, 0, 0)),
                  _const_spec((None, 1, SSM_WIDTH), (layer, 0, 0)),
                  _const_spec((None, ATTN_WIDTH, D_MODEL), (layer, 0, 0)),
                  _const_spec((None, SSM_WIDTH, D_MODEL), (layer, 0, 0)),
                  _const_spec((None, D_MODEL, D_MODEL), (layer, 0, 0))],
        out_specs=row(D_MODEL),
        out_shape=jax.ShapeDtypeStruct((m, D_MODEL), F32),
        compiler_params=_params("parallel"),
        name="merge",
    )(h, o, y, us, w["norm_mix"], w["w_in"], w["b_gate"], w["ssm_d"], w["glu_w"], w["glu_b"],
      w["w_attn_out"], w["w_ssm_out"], w["w_out"])


def _rope_tables(pos):
    half = HEAD_DIM // 2
    inv = ROPE_THETA ** (-jnp.arange(half, dtype=F32) / half)
    ang = pos.astype(F32)[:, None] * inv[None, :]
    cos = jnp.cos(ang)
    sin = jnp.sin(ang)
    reps = ATTN_WIDTH // HEAD_DIM
    cos = jnp.tile(jnp.concatenate([cos, cos], axis=1), (1, reps))
    sin = jnp.tile(jnp.concatenate([-sin, sin], axis=1), (1, reps))
    return cos, sin, cos.T, sin.T


def kernel(x_prompt, x_sample, cache_k, cache_v, state_ssm_re, state_ssm_im, page_table, norm_ffn1, ffn1_w_gate, ffn1_w_up, ffn1_w_down, norm_mix, w_in, b_gate, lambda_q1, lambda_k1, lambda_q2, lambda_k2, attn_subln, w_attn_out, ssm_a_re, ssm_a_im, ssm_log_dt, ssm_b_re, ssm_b_im, ssm_c_re, ssm_c_im, ssm_d, glu_w, glu_b, w_ssm_out, w_out, norm_ffn2, ffn2_w_gate, ffn2_w_up, ffn2_w_down, final_norm):
    batch, seq, _ = x_prompt.shape
    dec_batch, dec_seq, _ = x_sample.shape
    depth = w_in.shape[0]
    n_pool = cache_k.shape[1]
    past_len = page_table.shape[1] * PAGE_SIZE
    assert dec_seq == 1

    vec = lambda x: x.reshape(depth, 1, -1)
    bf = lambda x: x.astype(BF16)
    w_in_b = bf(w_in)
    w_kt = bf(jnp.swapaxes(w_in[:, :, ATTN_WIDTH:2 * ATTN_WIDTH], 1, 2))
    mw = dict(norm_mix=vec(norm_mix), w_in=w_in_b, b_gate=vec(b_gate), ssm_d=vec(ssm_d), glu_w=bf(glu_w),
              glu_b=vec(glu_b), w_attn_out=bf(w_attn_out), w_ssm_out=bf(w_ssm_out), w_out=bf(w_out))
    f1 = (vec(norm_ffn1), bf(ffn1_w_gate), bf(ffn1_w_up), bf(ffn1_w_down))
    f2 = (vec(norm_ffn2), bf(ffn2_w_gate), bf(ffn2_w_up), bf(ffn2_w_down))
    lam_rows = jnp.stack([lambda_q1, lambda_k1, lambda_q2, lambda_k2], axis=1)
    subln = vec(attn_subln)
    final_g = final_norm.reshape(1, D_MODEL)

    tabs_p = _rope_tables(jnp.arange(seq, dtype=jnp.int32))
    tabs_s = _rope_tables(jnp.full((dec_batch,), past_len, jnp.int32))
    cache_kt = jnp.transpose(cache_k, (0, 1, 3, 4, 2)).reshape(depth, n_pool, ATTN_WIDTH, PAGE_SIZE)
    cache_v2 = cache_v.reshape(depth, n_pool, PAGE_SIZE * N_HEADS, 2 * HEAD_DIM)

    mats = _ssm_prep(ssm_a_re, ssm_a_im, ssm_log_dt, ssm_b_re, ssm_b_im, ssm_c_re, ssm_c_im)
    sg_state = lambda x: x.reshape(dec_batch, N_SG, SG_STATE).transpose(1, 0, 2)
    sg_unstate = lambda x: x.transpose(1, 0, 2).reshape(dec_batch, N_GROUPS, STATE_DIM)

    xp = x_prompt.reshape(batch * seq, D_MODEL)
    xs = x_sample.reshape(dec_batch, D_MODEL)
    outs = [[] for _ in range(8)]
    kt_all = jnp.zeros((depth, batch, ATTN_WIDTH, seq), F32)
    v4_all = jnp.zeros((depth, batch, seq * N_HEADS, 2 * HEAD_DIM), F32)
    for l in range(depth):
        lam_init = 0.8 - 0.6 * math.exp(-0.3 * l)
        last = l == depth - 1

        hs = _ffn(xs, l, *f1)
        q, kt, _, v4, _, us_s, k_new = _inproj(hs.reshape(1, dec_batch, D_MODEL), l, mw["norm_mix"], w_in_b, w_kt,
                                               tabs_s, k_rows=True)
        v_new4 = v4.reshape(dec_batch, N_HEADS, 2 * HEAD_DIM)
        dec = (page_table, lam_rows, subln, q.reshape(dec_batch, 1, ATTN_WIDTH),
               k_new.reshape(dec_batch, 1, ATTN_WIDTH), v_new4, cache_kt, cache_v2, lam_init)

        h, o_lo = _ffn_decode(xp, l, *f1, dec, 0)
        q, kt_all, ktb, v4_all, vb, us = _inproj(h.reshape(batch, seq, D_MODEL), l, mw["norm_mix"], w_in_b, w_kt,
                                                 tabs_p, stacked=(kt_all, v4_all))
        o = _attn_prompt(q, ktb, vb, l, lam_rows, subln, lam_init)
        y, st = _ssm_prompt(us, l, mats)
        h = _merge(h, o.reshape(batch * seq, ATTN_WIDTH), y.reshape(batch * seq, SSM_WIDTH),
                   us.reshape(batch * seq, SSM_WIDTH), l, mw)
        xp, o_hi = _ffn_decode(h, l, *f2, dec, o_lo.shape[0], final_g=final_g if last else None)
        assert o_lo.shape[0] + o_hi.shape[0] == dec_batch
        outs[2].append(st[:, :, 0, :].reshape(batch, N_GROUPS, STATE_DIM))
        outs[3].append(st[:, :, 1, :].reshape(batch, N_GROUPS, STATE_DIM))

        o = jnp.concatenate([o_lo, o_hi], axis=0)
        us = us_s.reshape(dec_batch, SSM_WIDTH)
        y, h_re, h_im = _ssm_sample(us, sg_state(state_ssm_re[l]), sg_state(state_ssm_im[l]), l, mats)
        h = _merge(hs, o.reshape(dec_batch, ATTN_WIDTH), y, us, l, mw)
        xs = _ffn(h, l, *f2, final_g=final_g if last else None)
        outs[4].append(kt.reshape(2 * N_HEADS, HEAD_DIM, dec_batch).transpose(2, 0, 1)[:, None])
        outs[5].append(v_new4[:, None])
        outs[6].append(sg_unstate(h_re))
        outs[7].append(sg_unstate(h_im))

    stk = [jnp.stack(o) for o in outs[2:]]
    k_prompt = kt_all.reshape(depth, batch, 2 * N_HEADS, HEAD_DIM, seq).transpose(0, 1, 4, 2, 3)
    v_prompt = v4_all.reshape(depth, batch, seq, N_HEADS, 2 * HEAD_DIM)
    return (xp.reshape(batch, seq, D_MODEL), xs.reshape(dec_batch, dec_seq, D_MODEL),
            k_prompt, v_prompt, stk[0], stk[1], stk[2], stk[3], stk[4], stk[5])
```
